```python
import math
import jax, jax.numpy as jnp
from jax import lax
import numpy as np

D_MODEL = 1024
BATCH = 4
SEQ = 4096
DEPTH = 2
DEC_BATCH = 32
DEC_SEQ = 4
PAST_LEN = 16384
PAGE_SIZE = 128

N_MIXERS = 2
N_HEADS = 8
HEAD_DIM = 128
N_KV_HEADS = 2
N_IDX_HEADS = 4
IDX_DIM = 64
TOPK_MAX = 256
Q_BLOCK = 128
SSM_GROUP = 16
N_GROUPS = D_MODEL // SSM_GROUP
SSM_STATE = 64
D_FF = 2816
CONV_W = 3
NORM_EPS = 1e-6
NEG_INF = -1e30

Q_COLS = N_HEADS * HEAD_DIM
KV_COLS = N_KV_HEADS * HEAD_DIM
IQ_COLS = N_IDX_HEADS * IDX_DIM
IN_COLS = Q_COLS + 2 * KV_COLS + IQ_COLS + IDX_DIM + N_IDX_HEADS
SPLITS = [Q_COLS, Q_COLS + KV_COLS, Q_COLS + 2 * KV_COLS,
          Q_COLS + 2 * KV_COLS + IQ_COLS, Q_COLS + 2 * KV_COLS + IQ_COLS + IDX_DIM]

kernel_name = "dsa_s5_hybrid_step"


def rmsnorm(x, g):
    xf = x.astype(jnp.float32)
    y = xf * lax.rsqrt(jnp.mean(xf * xf, axis=-1, keepdims=True) + NORM_EPS)
    return (y * g.astype(jnp.float32)).astype(x.dtype)


def gather_rows(arr, idx):
    return jax.vmap(lambda a, i: a[i])(arr, idx)


def attn_project(h, w_in):
    b, t, _ = h.shape
    q, k, v, iq, ik, iw = jnp.split(h @ w_in, SPLITS, axis=-1)
    return (q.reshape(b, t, N_HEADS, HEAD_DIM), k.reshape(b, t, N_KV_HEADS, HEAD_DIM),
            v.reshape(b, t, N_KV_HEADS, HEAD_DIM), iq.reshape(b, t, N_IDX_HEADS, IDX_DIM), ik, iw)


def indexer_scores(iq, iw, ik):
    dots = jnp.einsum('bthd,bsd->bths', iq.astype(jnp.float32), ik.astype(jnp.float32)) * (IDX_DIM ** -0.5)
    w = iw.astype(jnp.float32) * (N_IDX_HEADS ** -0.5)
    return jnp.einsum('bths,bth->bts', jax.nn.relu(dots), w)


def sparse_attend(q, k_sel, v_sel, valid):
    b, t = q.shape[:2]
    qg = q.reshape(b, t, N_KV_HEADS, N_HEADS // N_KV_HEADS, HEAD_DIM).astype(jnp.float32)
    s = jnp.einsum('btgrd,btkgd->btgrk', qg, k_sel.astype(jnp.float32)) * (HEAD_DIM ** -0.5)
    s = jnp.where(valid[:, :, None, None, :], s, NEG_INF)
    p = jax.nn.softmax(s, axis=-1)
    o = jnp.einsum('btgrk,btkgd->btgrd', p, v_sel.astype(jnp.float32))
    return o.reshape(b, t, N_HEADS * HEAD_DIM).astype(q.dtype)


def dsa_prompt(h, w_in, w_out):
    b, s, _ = h.shape
    q, k, v, iq, ik, iw = attn_project(h, w_in)
    topk = min(TOPK_MAX, s // 4)
    key_pos = jnp.arange(s)

    def block(j):
        t0 = j * Q_BLOCK
        qb = lax.dynamic_slice_in_dim(q, t0, Q_BLOCK, axis=1)
        iqb = lax.dynamic_slice_in_dim(iq, t0, Q_BLOCK, axis=1)
        iwb = lax.dynamic_slice_in_dim(iw, t0, Q_BLOCK, axis=1)
        qpos = t0 + jnp.arange(Q_BLOCK)
        sc = indexer_scores(iqb, iwb, ik)
        sc = jnp.where((key_pos[None, :] <= qpos[:, None])[None], sc, NEG_INF)
        _, sel = lax.top_k(sc, topk)
        valid = sel <= qpos[None, :, None]
        return sparse_attend(qb, gather_rows(k, sel), gather_rows(v, sel), valid)

    o = lax.map(block, jnp.arange(s // Q_BLOCK))
    o = jnp.moveaxis(o, 0, 1).reshape(b, s, Q_COLS)
    return o @ w_out, k, v, ik


def dsa_sample(h, cache_k, cache_v, cache_idx_k, page_table, w_in, w_out):
    b, t, _ = h.shape
    past = page_table.shape[1] * PAGE_SIZE
    n_keys = past + t
    q, k, v, iq, ik, iw = attn_project(h, w_in)
    ik_past = cache_idx_k[page_table].reshape(b, past, IDX_DIM)
    ik_all = jnp.concatenate([ik_past, ik.astype(ik_past.dtype)], axis=1)
    qpos = past + jnp.arange(t)
    sc = indexer_scores(iq, iw, ik_all)
    sc = jnp.where((jnp.arange(n_keys)[None, :] <= qpos[:, None])[None], sc, NEG_INF)
    topk = min(TOPK_MAX, n_keys // 4)
    _, sel = lax.top_k(sc, topk)
    valid = sel <= qpos[None, :, None]
    in_past = (sel < past)[..., None, None]
    sel_past = jnp.minimum(sel, past - 1)
    phys = page_table[jnp.arange(b)[:, None, None], sel_past // PAGE_SIZE]
    off = sel_past % PAGE_SIZE
    sel_new = jnp.clip(sel - past, 0, t - 1)
    k_sel = jnp.where(in_past, cache_k[phys, off], gather_rows(k, sel_new))
    v_sel = jnp.where(in_past, cache_v[phys, off], gather_rows(v, sel_new))
    o = sparse_attend(q, k_sel, v_sel, valid)
    return o @ w_out, k, v, ik


def s5_mixer(h, h0_re, h0_im, a_re, a_im, log_dt, b_re, b_im, c_re, c_im, d_skip, w_glu, b_glu):
    f32 = jnp.float32
    bsz, t, _ = h.shape
    a_re, a_im = a_re.astype(f32), a_im.astype(f32)
    dt = jnp.exp(log_dt.astype(f32))[:, None]
    mag = jnp.exp(a_re * dt)
    lam_re, lam_im = mag * jnp.cos(a_im * dt), mag * jnp.sin(a_im * dt)
    den = a_re * a_re + a_im * a_im
    n_re, n_im = lam_re - 1.0, lam_im
    f_re = (n_re * a_re + n_im * a_im) / den
    f_im = (n_im * a_re - n_re * a_im) / den
    b_re, b_im = b_re.astype(f32), b_im.astype(f32)
    bb_re = f_re[..., None] * b_re - f_im[..., None] * b_im
    bb_im = f_re[..., None] * b_im + f_im[..., None] * b_re
    u = h.astype(f32).reshape(bsz, t, N_GROUPS, SSM_GROUP)
    bu_re = jnp.einsum('gpc,btgc->btgp', bb_re, u)
    bu_im = jnp.einsum('gpc,btgc->btgp', bb_im, u)
    h0_re, h0_im = h0_re.astype(f32), h0_im.astype(f32)
    bu_re = bu_re.at[:, 0].add(lam_re * h0_re - lam_im * h0_im)
    bu_im = bu_im.at[:, 0].add(lam_re * h0_im + lam_im * h0_re)
    la_re = jnp.broadcast_to(lam_re, bu_re.shape)
    la_im = jnp.broadcast_to(lam_im, bu_im.shape)

    def combine(x, y):
        ar, ai, br, bi = x
        cr, ci, dr, di = y
        return (cr * ar - ci * ai, cr * ai + ci * ar, cr * br - ci * bi + dr, cr * bi + ci * br + di)

    _, _, s_re, s_im = lax.associative_scan(combine, (la_re, la_im, bu_re, bu_im), axis=1)
    y = (jnp.einsum('gcp,btgp->btgc', c_re.astype(f32), s_re)
         - jnp.einsum('gcp,btgp->btgc', c_im.astype(f32), s_im)).reshape(bsz, t, D_MODEL)
    y = y + d_skip.astype(f32) * h.astype(f32)
    z = jax.nn.gelu(y) @ w_glu.astype(f32) + b_glu.astype(f32)
    val, gate = jnp.split(z, 2, axis=-1)
    return (val * jax.nn.sigmoid(gate)).astype(h.dtype), s_re[:, -1], s_im[:, -1]


def conv_ffn(h, buf, w_up, conv_w, conv_b, w_down):
    g, u = jnp.split(h @ w_up, 2, axis=-1)
    gx = jnp.concatenate([buf.astype(g.dtype), g], axis=1)
    gc = lax.conv_general_dilated(gx, conv_w[:, None, :].astype(gx.dtype), window_strides=(1,),
                                  padding='VALID', dimension_numbers=('NWC', 'WIO', 'NWC'),
                                  feature_group_count=D_FF) + conv_b.astype(gx.dtype)
    out = (jax.nn.silu(gc) * u) @ w_down
    return out, gx[:, -(CONV_W - 1):]


def setup_inputs(seed: int = 0) -> dict:
    key = jax.random.key(seed)
    ks = jax.random.split(key, 32)
    f32 = jnp.float32
    n_pages = PAST_LEN // PAGE_SIZE
    n_phys = (5 * DEC_BATCH * n_pages + 3) // 4

    def nrm(k, shape, scale=1.0):
        return jax.random.normal(k, shape, f32) * scale

    page_table = jax.random.permutation(ks[9], n_phys)[:DEC_BATCH * n_pages]
    page_table = page_table.reshape(DEC_BATCH, n_pages).astype(jnp.int32)
    a_im = jnp.pi * jnp.arange(SSM_STATE, dtype=f32)[None, :] + nrm(ks[12], (N_GROUPS, SSM_STATE), 0.01)
    return {
        'x_prompt': nrm(ks[0], (BATCH, SEQ, D_MODEL)),
        'x_sample': nrm(ks[1], (DEC_BATCH, DEC_SEQ, D_MODEL)),
        'cache_k': nrm(ks[2], (n_phys, PAGE_SIZE, N_KV_HEADS, HEAD_DIM)),
        'cache_v': nrm(ks[3], (n_phys, PAGE_SIZE, N_KV_HEADS, HEAD_DIM)),
        'cache_idx_k': nrm(ks[4], (n_phys, PAGE_SIZE, IDX_DIM)),
        'state_ssm_re': nrm(ks[5], (DEC_BATCH, N_GROUPS, SSM_STATE), 0.5),
        'state_ssm_im': nrm(ks[6], (DEC_BATCH, N_GROUPS, SSM_STATE), 0.5),
        'state_ffn_conv': nrm(ks[7], (DEPTH, DEC_BATCH, CONV_W - 1, D_FF)),
        'page_table': page_table,
        'w_attn_in': nrm(ks[8], (D_MODEL, IN_COLS), D_MODEL ** -0.5),
        'w_attn_out': nrm(ks[10], (Q_COLS, D_MODEL), Q_COLS ** -0.5),
        'ssm_a_re': -0.5 * jnp.exp(nrm(ks[11], (N_GROUPS, SSM_STATE), 0.01)),
        'ssm_a_im': a_im,
        'ssm_log_dt': jax.random.uniform(ks[13], (N_GROUPS,), f32, math.log(1e-3), math.log(1e-1)),
        'ssm_b_re': nrm(ks[14], (N_GROUPS, SSM_STATE, SSM_GROUP), (2 * SSM_GROUP) ** -0.5),
        'ssm_b_im': nrm(ks[15], (N_GROUPS, SSM_STATE, SSM_GROUP), (2 * SSM_GROUP) ** -0.5),
        'ssm_c_re': nrm(ks[16], (N_GROUPS, SSM_GROUP, SSM_STATE), (2 * SSM_STATE) ** -0.5),
        'ssm_c_im': nrm(ks[17], (N_GROUPS, SSM_GROUP, SSM_STATE), (2 * SSM_STATE) ** -0.5),
        'ssm_d': nrm(ks[18], (D_MODEL,)),
        'w_glu': nrm(ks[19], (D_MODEL, 2 * D_MODEL), D_MODEL ** -0.5),
        'b_glu': nrm(ks[20], (2 * D_MODEL,), 0.01),
        'norm_mixer': 1.0 + nrm(ks[21], (DEPTH, D_MODEL), 0.01),
        'norm_ffn': 1.0 + nrm(ks[22], (DEPTH, D_MODEL), 0.01),
        'w_ffn_up': nrm(ks[23], (DEPTH, D_MODEL, 2 * D_FF), D_MODEL ** -0.5),
        'ffn_conv_w': nrm(ks[24], (DEPTH, CONV_W, D_FF), CONV_W ** -0.5),
        'ffn_conv_b': nrm(ks[25], (DEPTH, D_FF), 0.01),
        'w_ffn_down': nrm(ks[26], (DEPTH, D_FF, D_MODEL), D_FF ** -0.5),
        'norm_final': 1.0 + nrm(ks[27], (D_MODEL,), 0.01),
    }


def reference(x_prompt, x_sample, cache_k, cache_v, cache_idx_k, state_ssm_re, state_ssm_im,
              state_ffn_conv, page_table, w_attn_in, w_attn_out, ssm_a_re, ssm_a_im, ssm_log_dt,
              ssm_b_re, ssm_b_im, ssm_c_re, ssm_c_im, ssm_d, w_glu, b_glu, norm_mixer, norm_ffn,
              w_ffn_up, ffn_conv_w, ffn_conv_b, w_ffn_down, norm_final):
    xp, xs = x_prompt, x_sample
    bp = xp.shape[0]
    conv_p_list, conv_s_list = [], []
    for i in range(DEPTH):
        hp, hs = rmsnorm(xp, norm_mixer[i]), rmsnorm(xs, norm_mixer[i])
        if i % N_MIXERS == 0:
            op, k_p, v_p, ik_p = dsa_prompt(hp, w_attn_in, w_attn_out)
            os_, k_s, v_s, ik_s = dsa_sample(hs, cache_k, cache_v, cache_idx_k, page_table,
                                             w_attn_in, w_attn_out)
        else:
            zeros = jnp.zeros((bp, N_GROUPS, SSM_STATE), jnp.float32)
            op, sre_p, sim_p = s5_mixer(hp, zeros, zeros, ssm_a_re, ssm_a_im, ssm_log_dt, ssm_b_re,
                                        ssm_b_im, ssm_c_re, ssm_c_im, ssm_d, w_glu, b_glu)
            os_, sre_s, sim_s = s5_mixer(hs, state_ssm_re, state_ssm_im, ssm_a_re, ssm_a_im, ssm_log_dt,
                                         ssm_b_re, ssm_b_im, ssm_c_re, ssm_c_im, ssm_d, w_glu, b_glu)
        xp, xs = xp + op, xs + os_
        buf_p = jnp.zeros((bp, CONV_W - 1, D_FF), xp.dtype)
        fp, cp = conv_ffn(rmsnorm(xp, norm_ffn[i]), buf_p, w_ffn_up[i], ffn_conv_w[i], ffn_conv_b[i], w_ffn_down[i])
        fs, cs = conv_ffn(rmsnorm(xs, norm_ffn[i]), state_ffn_conv[i], w_ffn_up[i], ffn_conv_w[i],
                          ffn_conv_b[i], w_ffn_down[i])
        xp, xs = xp + fp, xs + fs
        conv_p_list.append(cp)
        conv_s_list.append(cs)
    y_prompt = rmsnorm(xp, norm_final)
    y_sample = rmsnorm(xs, norm_final)
    conv_prompt = jnp.stack(conv_p_list, axis=0)
    conv_sample = jnp.stack(conv_s_list, axis=0)
    return (y_prompt, y_sample, k_p, v_p, ik_p, k_s, v_s, ik_s, sre_p, sim_p, sre_s, sim_s,
            conv_prompt, conv_sample)
```

```python
import functools
import math

import jax
import jax.numpy as jnp
from jax import lax
from jax.experimental import pallas as pl
from jax.experimental.pallas import tpu as pltpu

F32 = jnp.float32
BF16 = jnp.bfloat16

N_HEADS = 8
HEAD_DIM = 128
N_KV_HEADS = 2
N_IDX_HEADS = 4
IDX_DIM = 64
TOPK_MAX = 256
Q_BLOCK = 128
PAGE_SIZE = 128
SSM_GROUP = 16
SSM_STATE = 64
CONV_W = 3
NORM_EPS = 1e-6
NEG_INF = -1e30

LANE = 128
SUBLANE = 8
VMEM_LIMIT_BYTES = 56 * 1024 * 1024


def _params(*sem):
    return pltpu.CompilerParams(dimension_semantics=sem, vmem_limit_bytes=VMEM_LIMIT_BYTES)


def _rmsnorm(x, g):
    return x * lax.rsqrt(jnp.mean(x * x, axis=-1, keepdims=True) + NORM_EPS) * g


def _const_spec(shape):
    nd = len(shape)
    return pl.BlockSpec(shape, lambda *_: (0,) * nd)


def _norm_matmul_kernel(x_ref, g_ref, w_ref, o_ref):
    h = _rmsnorm(x_ref[...], g_ref[...])
    o_ref[...] = jnp.dot(h.astype(BF16), w_ref[...], preferred_element_type=F32)


def norm_matmul(x, g, w, tm):
    t, d = x.shape
    n = w.shape[1]
    return pl.pallas_call(
        _norm_matmul_kernel,
        grid=(t // tm,),
        in_specs=[pl.BlockSpec((tm, d), lambda i: (i, 0)), _const_spec((1, d)), _const_spec((d, n))],
        out_specs=pl.BlockSpec((tm, n), lambda i: (i, 0)),
        out_shape=jax.ShapeDtypeStruct((t, n), F32),
        compiler_params=_params("parallel"),
        name="norm_matmul",
    )(x, g.reshape(1, d), w)


def _matmul_residual_kernel(a_ref, w_ref, x_ref, o_ref):
    o_ref[...] = x_ref[...] + jnp.dot(a_ref[...], w_ref[...], preferred_element_type=F32)


def matmul_residual(a, w, x, tm):
    t, k = a.shape
    n = w.shape[1]
    return pl.pallas_call(
        _matmul_residual_kernel,
        grid=(t // tm,),
        in_specs=[pl.BlockSpec((tm, k), lambda i: (i, 0)), _const_spec((k, n)),
                  pl.BlockSpec((tm, n), lambda i: (i, 0))],
        out_specs=pl.BlockSpec((tm, n), lambda i: (i, 0)),
        out_shape=jax.ShapeDtypeStruct((t, n), F32),
        compiler_params=_params("parallel"),
        name="matmul_residual",
    )(a, w, x)


def _ffn_kernel(*refs, tm, nc, seq_tiles, seq_len, halo, final_norm):
    refs = list(refs)
    x_ref, gn_ref, wg_ref, wu_ref, cw_ref, cb_ref, wd_ref = refs[:7]
    pos = 7
    if halo:
        h1_ref, h2_ref = refs[pos:pos + 2]
        pos += 2
    if final_norm:
        gf_ref = refs[pos]
        pos += 1
    y_ref, gt_ref = refs[pos:pos + 2]
    h_scr, gbuf, acc = refs[pos + 2:pos + 5]
    if not halo:
        carry = refs[pos + 5]

    x = x_ref[...]
    h_scr[...] = _rmsnorm(x, gn_ref[...]).astype(BF16)
    acc[...] = jnp.zeros_like(acc)
    if halo:
        t_in_seq = lax.broadcasted_iota(jnp.int32, (tm, 1), 0) % seq_len
    else:
        first = (pl.program_id(0) % seq_tiles) == 0

    def chunk(c, _):
        hb = h_scr[...]
        g = jnp.dot(hb, wg_ref[c], preferred_element_type=F32)
        u = jnp.dot(hb, wu_ref[c], preferred_element_type=F32)
        gbuf[pl.ds(SUBLANE, tm), :] = g
        if halo:
            gbuf[pl.ds(0, SUBLANE), :] = jnp.zeros((SUBLANE, g.shape[1]), F32)
            gm1 = jnp.where(t_in_seq >= 1, gbuf[pl.ds(SUBLANE - 1, tm), :], h1_ref[c])
            gm2 = jnp.where(t_in_seq >= 2, gbuf[pl.ds(SUBLANE - 2, tm), :], h2_ref[c])
            gt_ref[0, c] = g
        else:
            gbuf[pl.ds(0, SUBLANE), :] = jnp.where(first, 0.0, carry[c])
            gm1 = gbuf[pl.ds(SUBLANE - 1, tm), :]
            gm2 = gbuf[pl.ds(SUBLANE - 2, tm), :]
            tail = g[tm - SUBLANE:, :]
            carry[c] = tail
            gt_ref[0, c] = tail
        cw = cw_ref[c]
        gc = cw[0:1, :] * gm2 + cw[1:2, :] * gm1 + cw[2:3, :] * g + cb_ref[c]
        act = gc * jax.nn.sigmoid(gc) * u
        acc[...] += jnp.dot(act.astype(BF16), wd_ref[c], preferred_element_type=F32)
        return 0

    lax.fori_loop(0, nc, chunk, 0)
    y = x + acc[...]
    if final_norm:
        y = _rmsnorm(y, gf_ref[...])
    y_ref[...] = y


def conv_ffn(x, gn, w_up, conv_w, conv_b, w_down, *, tm, seq_len, halo=None, g_final=None, fc=256):
    t, d = x.shape
    f = w_down.shape[0]
    nc = f // fc
    wg = w_up[:, :f].astype(BF16).reshape(d, nc, fc).transpose(1, 0, 2)
    wu = w_up[:, f:].astype(BF16).reshape(d, nc, fc).transpose(1, 0, 2)
    wd = w_down.astype(BF16).reshape(nc, fc, d)
    cw = jnp.pad(conv_w, ((0, SUBLANE - CONV_W), (0, 0))).reshape(SUBLANE, nc, fc).transpose(1, 0, 2)
    cb = conv_b.reshape(nc, 1, fc)
    n_tiles = t // tm
    args = [x, gn.reshape(1, d), wg, wu, cw, cb, wd]
    in_specs = [pl.BlockSpec((tm, d), lambda i: (i, 0)), _const_spec((1, d)),
                _const_spec((nc, d, fc)), _const_spec((nc, d, fc)), _const_spec((nc, SUBLANE, fc)),
                _const_spec((nc, 1, fc)), _const_spec((nc, fc, d))]
    scratch = [pltpu.VMEM((tm, d), BF16), pltpu.VMEM((tm + SUBLANE, fc), F32), pltpu.VMEM((tm, d), F32)]
    if halo is not None:
        assert n_tiles == 1 and seq_len >= CONV_W - 1
        n_seq = t // seq_len
        z = jnp.zeros((n_seq, seq_len, f), F32)
        h1 = z.at[:, 0].set(halo[:, 1]).reshape(t, nc, fc).transpose(1, 0, 2)
        h2 = z.at[:, 0].set(halo[:, 0]).at[:, 1].set(halo[:, 1]).reshape(t, nc, fc).transpose(1, 0, 2)
        args += [h1, h2]
        in_specs += [_const_spec((nc, tm, fc)), _const_spec((nc, tm, fc))]
        tail_rows = tm
        seq_tiles = 1
    else:
        assert seq_len % tm == 0
        tail_rows = SUBLANE
        seq_tiles = seq_len // tm
        scratch.append(pltpu.VMEM((nc, SUBLANE, fc), F32))
    if g_final is not None:
        args.append(g_final.reshape(1, d))
        in_specs.append(_const_spec((1, d)))
    kern = functools.partial(_ffn_kernel, tm=tm, nc=nc, seq_tiles=seq_tiles, seq_len=seq_len,
                             halo=halo is not None, final_norm=g_final is not None)
    y, gt = pl.pallas_call(
        kern,
        grid=(n_tiles,),
        in_specs=in_specs,
        out_specs=[pl.BlockSpec((tm, d), lambda i: (i, 0)),
                   pl.BlockSpec((1, nc, tail_rows, fc), lambda i: (i, 0, 0, 0))],
        out_shape=[jax.ShapeDtypeStruct((t, d), F32),
                   jax.ShapeDtypeStruct((n_tiles, nc, tail_rows, fc), F32)],
        scratch_shapes=scratch,
        compiler_params=_params("arbitrary"),
        name="conv_ffn",
    )(*args)
    gt = gt.transpose(0, 2, 1, 3).reshape(n_tiles, tail_rows, f)
    return y, gt


S5_LANE_CHUNK = 512
S5_IN_BLOCK = 256
S5_OUT_GROUPS = 8


def _s5_kernel(*refs, tm, nseq, nsteps, chained):
    refs = list(refs)
    (x_ref, gn_ref, bb_ref, lam_ref, pw_ref, cre_ref, cim_ref, d_ref, wglu_ref, bglu_ref) = refs[:10]
    pos = 10
    if not chained:
        s0_ref = refs[pos]
        pos += 1
    y_ref, sfin_ref = refs[pos:pos + 2]
    xp, hp, sre, sim, yacc = refs[pos + 2:pos + 7]
    if chained:
        cin, st_carry = refs[pos + 7:pos + 9]
    n_state = sre.shape[1]
    d_model = x_ref.shape[-1]

    xp[...] = x_ref[0]
    h = _rmsnorm(xp[...], gn_ref[...])
    hp[...] = h.astype(BF16)

    n_in_blocks = d_model // S5_IN_BLOCK
    wcols = n_state // n_in_blocks
    for blk in range(n_in_blocks):
        bu = jnp.dot(hp[:, blk * S5_IN_BLOCK:(blk + 1) * S5_IN_BLOCK], bb_ref[blk],
                     preferred_element_type=F32)
        sre[:, blk * wcols:(blk + 1) * wcols] = bu[:, :wcols]
        sim[:, blk * wcols:(blk + 1) * wcols] = bu[:, wcols:]

    if chained:
        @pl.when(pl.program_id(1) == 0)
        def _():
            st_carry[...] = jnp.zeros_like(st_carry)

    lc = S5_LANE_CHUNK
    for c0 in range(0, n_state, lc):
        cols = slice(c0, c0 + lc)
        lr = jnp.broadcast_to(lam_ref[0:1, cols], (nseq, lc))
        li = jnp.broadcast_to(lam_ref[1:2, cols], (nseq, lc))
        if chained:
            init = (jnp.zeros((nseq, lc), F32), jnp.zeros((nseq, lc), F32))
        else:
            init = (s0_ref[0, :, cols], s0_ref[1, :, cols])

        def step(i, carry, cols=cols, lr=lr, li=li):
            cr, ci = carry
            rows = pl.ds(pl.multiple_of(i * nseq, nseq), nseq)
            nr = lr * cr - li * ci + sre[rows, cols]
            ni = lr * ci + li * cr + sim[rows, cols]
            sre[rows, cols] = nr
            sim[rows, cols] = ni
            return nr, ni

        cr, ci = lax.fori_loop(0, nsteps, step, init, unroll=4)

        if not chained:
            sfin_ref[0, :, cols] = cr
            sfin_ref[1, :, cols] = ci
        else:
            last = (nsteps - 1) * nseq
            pr_l = pw_ref[0, nsteps - 1:nsteps, cols]
            pi_l = pw_ref[1, nsteps - 1:nsteps, cols]
            c_r = st_carry[0:1, cols]
            c_i = st_carry[1:2, cols]
            for seq in range(nseq):
                cin[0, seq:seq + 1, :] = c_r
                cin[1, seq:seq + 1, :] = c_i
                e_r = sre[last + seq:last + seq + 1, cols]
                e_i = sim[last + seq:last + seq + 1, cols]
                c_r, c_i = pr_l * c_r - pi_l * c_i + e_r, pr_l * c_i + pi_l * c_r + e_i
            st_carry[0:1, cols] = c_r
            st_carry[1:2, cols] = c_i
            in_r = cin[0]
            in_i = cin[1]

            def fix(i, _, cols=cols, in_r=in_r, in_i=in_i):
                rows = pl.ds(pl.multiple_of(i * nseq, nseq), nseq)
                pr = jnp.broadcast_to(pw_ref[0, pl.ds(i, 1), cols], (nseq, lc))
                pi = jnp.broadcast_to(pw_ref[1, pl.ds(i, 1), cols], (nseq, lc))
                sre[rows, cols] = sre[rows, cols] + (pr * in_r - pi * in_i)
                sim[rows, cols] = sim[rows, cols] + (pr * in_i + pi * in_r)
                return 0

            lax.fori_loop(0, nsteps, fix, 0, unroll=4)

    if chained:
        sfin_ref[0] = st_carry[...]

    kcols = S5_OUT_GROUPS * SSM_STATE
    ncols = S5_OUT_GROUPS * SSM_GROUP
    for k in range(n_state // kcols):
        yk = jnp.dot(sre[:, k * kcols:(k + 1) * kcols].astype(BF16), cre_ref[k], preferred_element_type=F32)
        yk += jnp.dot(sim[:, k * kcols:(k + 1) * kcols].astype(BF16), cim_ref[k], preferred_element_type=F32)
        yacc[:, k * ncols:(k + 1) * ncols] = yk
    y = yacc[...] + d_ref[...] * _rmsnorm(xp[...], gn_ref[...])
    z = jnp.dot(jax.nn.gelu(y).astype(BF16), wglu_ref[...], preferred_element_type=F32) + bglu_ref[...]
    out = xp[...] + z[:, :d_model] * jax.nn.sigmoid(z[:, d_model:])
    y_ref[0] = out


def _s5_discretize(a_re, a_im, log_dt, b_re, b_im):
    a_re, a_im = a_re.astype(F32), a_im.astype(F32)
    dt = jnp.exp(log_dt.astype(F32))[:, None]
    mag = jnp.exp(a_re * dt)
    lam_re, lam_im = mag * jnp.cos(a_im * dt), mag * jnp.sin(a_im * dt)
    den = a_re * a_re + a_im * a_im
    n_re, n_im = lam_re - 1.0, lam_im
    f_re = (n_re * a_re + n_im * a_im) / den
    f_im = (n_im * a_re - n_re * a_im) / den
    b_re, b_im = b_re.astype(F32), b_im.astype(F32)
    bb_re = f_re[..., None] * b_re - f_im[..., None] * b_im
    bb_im = f_re[..., None] * b_im + f_im[..., None] * b_re
    return lam_re, lam_im, bb_re, bb_im


def _s5_weights(lam_re, lam_im, bb_re, bb_im, c_re, c_im, nsteps):
    n_groups = lam_re.shape[0]
    n_state = n_groups * SSM_STATE
    pr, pi = [lam_re.reshape(-1)], [lam_im.reshape(-1)]
    for _ in range(nsteps - 1):
        pr, pi = pr + [pr[-1] * pr[0] - pi[-1] * pi[0]], pi + [pr[-1] * pi[0] + pi[-1] * pr[0]]
    pw = jnp.stack([jnp.stack(pr), jnp.stack(pi)])
    lam = jnp.stack([lam_re.reshape(-1), lam_im.reshape(-1)])
    gpb = S5_IN_BLOCK // SSM_GROUP
    nb = n_groups // gpb
    eye = jnp.eye(gpb, dtype=F32)

    def bdiag_in(bb):
        bb = bb.reshape(nb, gpb, SSM_STATE, SSM_GROUP)
        return jnp.einsum('ngpc,gh->ngchp', bb, eye).reshape(nb, gpb * SSM_GROUP, gpb * SSM_STATE)

    bmat = jnp.concatenate([bdiag_in(bb_re), bdiag_in(bb_im)], axis=-1).astype(BF16)
    go = S5_OUT_GROUPS
    eye_o = jnp.eye(go, dtype=F32)

    def bdiag_out(cc):
        cc = cc.astype(F32).reshape(n_groups // go, go, SSM_GROUP, SSM_STATE)
        return jnp.einsum('ngcp,gh->ngphc', cc, eye_o).reshape(n_groups // go, go * SSM_STATE, go * SSM_GROUP)

    return lam, pw, bmat, bdiag_out(c_re).astype(BF16), bdiag_out(-c_im.astype(F32)).astype(BF16)


def s5_layer(x, gn, disc, c_re, c_im, d_skip, w_glu, b_glu, *, tm=None, s0=None):
    b, s, d = x.shape
    lam_re, lam_im, bb_re, bb_im = disc
    n_groups = lam_re.shape[0]
    n_state = n_groups * SSM_STATE
    chained = s0 is None
    if chained:
        nseq, nsteps = SUBLANE, tm // SUBLANE
        grid = (b, s // tm)
        xin = x.reshape(b, s // tm, nseq, nsteps, d).transpose(0, 1, 3, 2, 4).reshape(b, s, d)
        x_spec = pl.BlockSpec((1, tm, d), lambda i, j: (i, j, 0))
        sfin_spec = pl.BlockSpec((1, 2, n_state), lambda i, j: (i, 0, 0))
        sfin_shape = jax.ShapeDtypeStruct((b, 2, n_state), F32)
    else:
        nseq, nsteps, tm = b, s, b * s
        grid = (1, 1)
        xin = x.transpose(1, 0, 2).reshape(1, tm, d)
        x_spec = pl.BlockSpec((1, tm, d), lambda i, j: (0, 0, 0))
        sfin_spec = _const_spec((2, nseq, n_state))
        sfin_shape = jax.ShapeDtypeStruct((2, nseq, n_state), F32)
    lam, pw, bmat, cre, cim = _s5_weights(lam_re, lam_im, bb_re, bb_im, c_re, c_im, nsteps)
    args = [xin, gn.reshape(1, d), bmat, lam, pw, cre, cim, d_skip.reshape(1, d).astype(F32),
            w_glu.astype(BF16), b_glu.reshape(1, 2 * d).astype(F32)]
    in_specs = [x_spec, _const_spec((1, d)), _const_spec(bmat.shape), _const_spec(lam.shape),
                _const_spec(pw.shape), _const_spec(cre.shape), _const_spec(cim.shape),
                _const_spec((1, d)), _const_spec((d, 2 * d)), _const_spec((1, 2 * d))]
    scratch = [pltpu.VMEM((tm, d), F32), pltpu.VMEM((tm, d), BF16), pltpu.VMEM((tm, n_state), F32),
               pltpu.VMEM((tm, n_state), F32), pltpu.VMEM((tm, d), F32)]
    if chained:
        scratch += [pltpu.VMEM((2, nseq, S5_LANE_CHUNK), F32), pltpu.VMEM((2, n_state), F32)]
    else:
        args.append(jnp.stack([s0[0].reshape(b, n_state), s0[1].reshape(b, n_state)]).astype(F32))
        in_specs.append(_const_spec((2, nseq, n_state)))
    kern = functools.partial(_s5_kernel, tm=tm, nseq=nseq, nsteps=nsteps, chained=chained)
    y, sfin = pl.pallas_call(
        kern,
        grid=grid,
        in_specs=in_specs,
        out_specs=[x_spec, sfin_spec],
        out_shape=[jax.ShapeDtypeStruct(xin.shape, F32), sfin_shape],
        scratch_shapes=scratch,
        compiler_params=_params("arbitrary", "arbitrary"),
        name="s5_layer",
    )(*args)
    if chained:
        y = y.reshape(b, s // tm, nsteps, nseq, d).transpose(0, 1, 3, 2, 4).reshape(b, s, d)
        return (y, sfin[:, 0].reshape(b, n_groups, SSM_STATE), sfin[:, 1].reshape(b, n_groups, SSM_STATE))
    y = y.reshape(s, b, d).transpose(1, 0, 2)
    return y, sfin[0].reshape(b, n_groups, SSM_STATE), sfin[1].reshape(b, n_groups, SSM_STATE)


INT_MIN = -2 ** 31
N_KEY_BITS = 32


def _candidate(prefix, i):
    cand = prefix | lax.shift_left(jnp.int32(1), 31 - i)
    u = cand ^ jnp.int32(INT_MIN)
    bits = jnp.where(u >= 0, u, u ^ jnp.int32(0x7FFFFFFF))
    return cand, lax.bitcast_convert_type(bits, F32)


def _key_to_float(prefix):
    u = prefix ^ jnp.int32(INT_MIN)
    return lax.bitcast_convert_type(jnp.where(u >= 0, u, u ^ jnp.int32(0x7FFFFFFF)), F32)


def _dsa_prompt_kernel(q_ref, k_ref, vt_ref, iq_ref, ik_ref, iw_ref, o_ref, sc, bias, m_scr, l_scr, acc,
                       *, topk):
    qb = Q_BLOCK
    rep = N_HEADS // N_KV_HEADS
    j = pl.program_id(1)
    nkb = j + 1

    def key_rows(kb):
        return pl.ds(pl.multiple_of(kb * qb, qb), qb)

    kpos = lax.broadcasted_iota(jnp.int32, (qb, qb), 0)
    qpos = lax.broadcasted_iota(jnp.int32, (qb, qb), 1)
    causal = kpos <= qpos
    tri = jnp.where(qpos <= kpos, 1.0, 0.0).astype(BF16)

    iq = iq_ref[0].reshape(N_IDX_HEADS * qb, IDX_DIM)
    iw = iw_ref[0]

    def score_blk(kb, _):
        dots = lax.dot_general(ik_ref[0, key_rows(kb), :], iq, (((1,), (1,)), ((), ())),
                               preferred_element_type=F32)
        s = jnp.zeros((qb, qb), F32)
        for h in range(N_IDX_HEADS):
            s = s + jnp.maximum(dots[:, h * qb:(h + 1) * qb], 0.0) * iw[h:h + 1, :]
        sc[key_rows(kb), :] = s
        return 0

    lax.fori_loop(0, nkb, score_blk, 0)
    sc[key_rows(j), :] = jnp.where(causal, sc[key_rows(j), :], NEG_INF)

    @pl.when(nkb * qb <= topk)
    def _():
        def fill(kb, _):
            bias[key_rows(kb), :] = jnp.zeros((qb, qb), F32)
            return 0

        lax.fori_loop(0, j, fill, 0)
        bias[key_rows(j), :] = jnp.where(causal, 0.0, NEG_INF)

    @pl.when(nkb * qb > topk)
    def _():
        kf = jnp.float32(topk)

        def count(pred):
            def blk(kb, cnt):
                ind = jnp.where(pred(sc[key_rows(kb), :]), 1.0, 0.0)
                return cnt + jnp.sum(ind.reshape(qb // SUBLANE, SUBLANE, qb), axis=0)

            cnt = lax.fori_loop(0, nkb, blk, jnp.zeros((SUBLANE, qb), F32))
            return jnp.sum(cnt, axis=0, keepdims=True)

        def bit_body(i, prefix):
            cand, cf = _candidate(prefix, i)
            return jnp.where(count(lambda s: s >= cf) >= kf, cand, prefix)

        thr = _key_to_float(lax.fori_loop(0, N_KEY_BITS, bit_body, jnp.zeros((1, qb), jnp.int32)))
        need = kf - count(lambda s: s > thr)

        def select_blk(kb, run):
            s = sc[key_rows(kb), :]
            tie = s == thr
            pre = jnp.dot(tri, jnp.where(tie, 1.0, 0.0).astype(BF16), preferred_element_type=F32)
            sel = (s > thr) | (tie & (pre + run <= need))
            bias[key_rows(kb), :] = jnp.where(sel, 0.0, NEG_INF)
            return run + pre[qb - 1:qb, :]

        lax.fori_loop(0, nkb, select_blk, jnp.zeros((1, qb), F32))

    m_scr[...] = jnp.full(m_scr.shape, NEG_INF, F32)
    l_scr[...] = jnp.zeros_like(l_scr)
    acc[...] = jnp.zeros_like(acc)

    def attn_blk(kb, _):
        b1 = bias[key_rows(kb), :]
        bt = jnp.concatenate([b1] * rep, axis=1)
        for g in range(N_KV_HEADS):
            qg = q_ref[0, g * rep:(g + 1) * rep].reshape(rep * qb, HEAD_DIM)
            st = lax.dot_general(k_ref[0, g, key_rows(kb), :], qg, (((1,), (1,)), ((), ())),
                                 preferred_element_type=F32) + bt
            m_old = m_scr[g]
            m_new = jnp.maximum(m_old, jnp.max(st, axis=0, keepdims=True))
            alpha = jnp.exp(m_old - m_new)
            p = jnp.exp(st - m_new)
            l_scr[g] = alpha * l_scr[g] + jnp.sum(p, axis=0, keepdims=True)
            acc[g] = alpha * acc[g] + jnp.dot(vt_ref[0, g, kb], p.astype(BF16), preferred_element_type=F32)
            m_scr[g] = m_new
        return 0

    lax.fori_loop(0, nkb, attn_blk, 0)
    for g in range(N_KV_HEADS):
        ot = acc[g] / l_scr[g]
        for r in range(rep):
            hd = g * rep + r
            o_ref[0, :, hd * HEAD_DIM:(hd + 1) * HEAD_DIM] = ot[:, r * qb:(r + 1) * qb].T.astype(o_ref.dtype)


def dsa_prompt_attention(q, k, v, iq, ik, iw):
    b, s = q.shape[:2]
    topk = min(TOPK_MAX, s // 4)
    assert topk % Q_BLOCK == 0 and s % Q_BLOCK == 0
    nkb = s // Q_BLOCK
    q_hm = (q * HEAD_DIM ** -0.5).astype(BF16).transpose(0, 2, 1, 3)
    k_hm = k.astype(BF16).transpose(0, 2, 1, 3)
    vt = v.astype(BF16).reshape(b, nkb, Q_BLOCK, N_KV_HEADS, HEAD_DIM).transpose(0, 3, 1, 4, 2)
    iq_hm = iq.astype(BF16).transpose(0, 2, 1, 3)
    iw_t = (iw * (IDX_DIM ** -0.5 * N_IDX_HEADS ** -0.5)).transpose(0, 2, 1)
    rep = N_HEADS // N_KV_HEADS
    return pl.pallas_call(
        functools.partial(_dsa_prompt_kernel, topk=topk),
        grid=(b, nkb),
        in_specs=[
            pl.BlockSpec((1, N_HEADS, Q_BLOCK, HEAD_DIM), lambda i, j: (i, 0, j, 0)),
            pl.BlockSpec((1, N_KV_HEADS, s, HEAD_DIM), lambda i, j: (i, 0, 0, 0)),
            pl.BlockSpec((1, N_KV_HEADS, nkb, HEAD_DIM, Q_BLOCK), lambda i, j: (i, 0, 0, 0, 0)),
            pl.BlockSpec((1, N_IDX_HEADS, Q_BLOCK, IDX_DIM), lambda i, j: (i, 0, j, 0)),
            pl.BlockSpec((1, s, IDX_DIM), lambda i, j: (i, 0, 0)),
            pl.BlockSpec((1, N_IDX_HEADS, Q_BLOCK), lambda i, j: (i, 0, j)),
        ],
        out_specs=pl.BlockSpec((1, Q_BLOCK, N_HEADS * HEAD_DIM), lambda i, j: (i, j, 0)),
        out_shape=jax.ShapeDtypeStruct((b, s, N_HEADS * HEAD_DIM), BF16),
        scratch_shapes=[pltpu.VMEM((s, Q_BLOCK), F32), pltpu.VMEM((s, Q_BLOCK), F32),
                        pltpu.VMEM((N_KV_HEADS, 1, rep * Q_BLOCK), F32),
                        pltpu.VMEM((N_KV_HEADS, 1, rep * Q_BLOCK), F32),
                        pltpu.VMEM((N_KV_HEADS, HEAD_DIM, rep * Q_BLOCK), F32)],
        compiler_params=_params("arbitrary", "arbitrary"),
        name="dsa_prompt",
    )(q_hm, k_hm, vt, iq_hm, ik.astype(BF16), iw_t)


PAGES_PER_CHUNK = 16
T_PAD = SUBLANE


def _dsa_sample_kernel(pt_ref, q_ref, iq_ref, iw_ref, knew_ref, vnew_ref, iknew_ref,
                       cik_hbm, ck_hbm, cv_hbm, o_ref,
                       ikbuf, kvbuf, sc, bias, lg, sem_ik, sem_kv, *, n_pages, n_new, topk):
    b = pl.program_id(0)
    cpp = PAGES_PER_CHUNK
    nch = n_pages // cpp
    cw = cpp * PAGE_SIZE
    rep = N_HEADS // N_KV_HEADS
    rows = rep * T_PAD

    def page_copy(src_hbm, buf, sem, chunk, slot, i):
        page = pt_ref[b, chunk * cpp + i]
        return pltpu.make_async_copy(src_hbm.at[page], buf.at[slot, pl.ds(i * PAGE_SIZE, PAGE_SIZE), :],
                                     sem.at[slot])

    def fetch(src_hbm, buf, sem, chunk, slot):
        for i in range(cpp):
            page_copy(src_hbm, buf, sem, chunk, slot, i).start()

    def wait(src_hbm, buf, sem, chunk, slot):
        for i in range(cpp):
            page_copy(src_hbm, buf, sem, chunk, slot, i).wait()

    def sweep(src_hbm, buf, sem, body):
        fetch(src_hbm, buf, sem, 0, 0)

        def step(c, _):
            slot = c % 2

            @pl.when(c + 1 < nch)
            def _():
                fetch(src_hbm, buf, sem, c + 1, 1 - slot)

            wait(src_hbm, buf, sem, c, slot)
            body(c, slot)
            return 0

        lax.fori_loop(0, nch, step, 0)

    iq = iq_ref[0]
    iw = iw_ref[0]

    def scores_of(ikc):
        dots = lax.dot_general(iq, ikc, (((1,), (1,)), ((), ())), preferred_element_type=F32)
        s = jnp.zeros((T_PAD, ikc.shape[0]), F32)
        for h in range(N_IDX_HEADS):
            w = iw[h * T_PAD:(h + 1) * T_PAD, 0:1]
            s = s + jnp.maximum(dots[h * T_PAD:(h + 1) * T_PAD, :], 0.0) * w
        return s

    def score_chunk(c, slot):
        sc[c] = scores_of(ikbuf[slot].astype(BF16))

    sweep(cik_hbm, ikbuf, sem_ik, score_chunk)
    kcol = lax.broadcasted_iota(jnp.int32, (T_PAD, PAGE_SIZE), 1)
    trow = lax.broadcasted_iota(jnp.int32, (T_PAD, PAGE_SIZE), 0)
    new_ok = (kcol <= trow) & (kcol < n_new)
    sc[nch] = jnp.full((T_PAD, cw), NEG_INF, F32)
    sc[nch, :, 0:PAGE_SIZE] = jnp.where(new_ok, scores_of(iknew_ref[0]), NEG_INF)

    kf = jnp.float32(topk)

    def count(pred):
        def blk(c, cnt):
            return cnt + jnp.where(pred(sc[c]), 1.0, 0.0)

        cnt = lax.fori_loop(0, nch + 1, blk, jnp.zeros((T_PAD, cw), F32))
        return jnp.sum(cnt, axis=1, keepdims=True)

    def bit_body(i, prefix):
        cand, cf = _candidate(prefix, i)
        return jnp.where(count(lambda s: s >= cf) >= kf, cand, prefix)

    thr = _key_to_float(lax.fori_loop(0, N_KEY_BITS, bit_body, jnp.zeros((T_PAD, 1), jnp.int32)))
    need = kf - count(lambda s: s > thr)
    ki = lax.broadcasted_iota(jnp.int32, (LANE, LANE), 0)
    kj = lax.broadcasted_iota(jnp.int32, (LANE, LANE), 1)
    tri = jnp.where(ki <= kj, 1.0, 0.0).astype(BF16)

    def select_blk(c, run):
        s = sc[c]
        for t in range(cw // LANE):
            st = s[:, t * LANE:(t + 1) * LANE]
            tie = st == thr
            pre = jnp.dot(jnp.where(tie, 1.0, 0.0).astype(BF16), tri, preferred_element_type=F32)
            sel = (st > thr) | (tie & (pre + run <= need))
            bias[c, :, t * LANE:(t + 1) * LANE] = jnp.where(sel, 0.0, NEG_INF)
            run = run + pre[:, LANE - 1:LANE]
        return run

    lax.fori_loop(0, nch + 1, select_blk, jnp.zeros((T_PAD, 1), F32))

    def logits_of(kc, g, bias_c):
        qg = q_ref[0, g]
        lgt = lax.dot_general(qg, kc[:, g * HEAD_DIM:(g + 1) * HEAD_DIM], (((1,), (1,)), ((), ())),
                              preferred_element_type=F32)
        return lgt + jnp.concatenate([bias_c] * rep, axis=0)

    def logit_chunk(c, slot):
        kc = kvbuf[slot].astype(BF16)
        for g in range(N_KV_HEADS):
            lg[g, c] = logits_of(kc, g, bias[c])

    sweep(ck_hbm, kvbuf, sem_kv, logit_chunk)
    for g in range(N_KV_HEADS):
        lg[g, nch] = jnp.full((rows, cw), NEG_INF, F32)
        lg[g, nch, :, 0:PAGE_SIZE] = logits_of(knew_ref[0], g, bias[nch, :, 0:PAGE_SIZE])

    denom = []
    for g in range(N_KV_HEADS):
        mx = lax.fori_loop(0, nch + 1, lambda c, m, g=g: jnp.maximum(m, lg[g, c]),
                           jnp.full((rows, cw), NEG_INF, F32))
        mx = jnp.max(mx, axis=1, keepdims=True)

        def exp_blk(c, tot, g=g, mx=mx):
            p = jnp.exp(lg[g, c] - mx)
            lg[g, c] = p
            return tot + p

        tot = lax.fori_loop(0, nch + 1, exp_blk, jnp.zeros((rows, cw), F32))
        denom.append(jnp.sum(tot, axis=1, keepdims=True))
        o_ref[0, g] = jnp.dot(lg[g, nch, :, 0:PAGE_SIZE].astype(BF16),
                              vnew_ref[0][:, g * HEAD_DIM:(g + 1) * HEAD_DIM], preferred_element_type=F32)

    def value_chunk(c, slot):
        vc = kvbuf[slot].astype(BF16)
        for g in range(N_KV_HEADS):
            o_ref[0, g] += jnp.dot(lg[g, c].astype(BF16), vc[:, g * HEAD_DIM:(g + 1) * HEAD_DIM],
                                   preferred_element_type=F32)

    sweep(cv_hbm, kvbuf, sem_kv, value_chunk)
    for g in range(N_KV_HEADS):
        o_ref[0, g] = o_ref[0, g] / denom[g]


def dsa_sample_attention(q, k, v, iq, ik, iw, cache_k, cache_v, cache_idx_k, page_table):
    b, t = q.shape[:2]
    n_pages = page_table.shape[1]
    n_phys = cache_k.shape[0]
    n_keys = n_pages * PAGE_SIZE + t
    topk = min(TOPK_MAX, n_keys // 4)
    rep = N_HEADS // N_KV_HEADS
    assert t <= T_PAD and n_pages % PAGES_PER_CHUNK == 0 and n_pages * PAGE_SIZE >= topk
    nch = n_pages // PAGES_PER_CHUNK
    cw = PAGES_PER_CHUNK * PAGE_SIZE

    def pad_t(x):
        return jnp.pad(x, ((0, 0), (0, T_PAD - t)) + ((0, 0),) * (x.ndim - 2))

    def pad_page(x):
        return jnp.pad(x, ((0, 0), (0, PAGE_SIZE - t), (0, 0)))

    kvc = N_KV_HEADS * HEAD_DIM
    q_s = pad_t((q * HEAD_DIM ** -0.5).astype(BF16)).reshape(b, T_PAD, N_KV_HEADS, rep, HEAD_DIM)
    q_s = q_s.transpose(0, 2, 3, 1, 4).reshape(b, N_KV_HEADS, rep * T_PAD, HEAD_DIM)
    iq_s = pad_t(iq.astype(BF16)).transpose(0, 2, 1, 3).reshape(b, N_IDX_HEADS * T_PAD, IDX_DIM)
    iw_s = pad_t(iw * (IDX_DIM ** -0.5 * N_IDX_HEADS ** -0.5)).transpose(0, 2, 1)
    iw_s = jnp.broadcast_to(iw_s.reshape(b, N_IDX_HEADS * T_PAD, 1), (b, N_IDX_HEADS * T_PAD, LANE))
    k_new = pad_page(k.reshape(b, t, kvc).astype(BF16))
    v_new = pad_page(v.reshape(b, t, kvc).astype(BF16))
    ik_new = pad_page(ik.astype(BF16))

    def bspec(shape):
        nd = len(shape)
        return pl.BlockSpec((1,) + shape, lambda i, pt: (i,) + (0,) * nd)

    kern = functools.partial(_dsa_sample_kernel, n_pages=n_pages, n_new=t, topk=topk)
    o = pl.pallas_call(
        kern,
        grid_spec=pltpu.PrefetchScalarGridSpec(
            num_scalar_prefetch=1,
            grid=(b,),
            in_specs=[bspec((N_KV_HEADS, rep * T_PAD, HEAD_DIM)), bspec((N_IDX_HEADS * T_PAD, IDX_DIM)),
                      bspec((N_IDX_HEADS * T_PAD, LANE)), bspec((PAGE_SIZE, kvc)), bspec((PAGE_SIZE, kvc)),
                      bspec((PAGE_SIZE, IDX_DIM)),
                      pl.BlockSpec(memory_space=pl.ANY), pl.BlockSpec(memory_space=pl.ANY),
                      pl.BlockSpec(memory_space=pl.ANY)],
            out_specs=bspec((N_KV_HEADS, rep * T_PAD, HEAD_DIM)),
            scratch_shapes=[pltpu.VMEM((2, cw, IDX_DIM), F32), pltpu.VMEM((2, cw, kvc), F32),
                            pltpu.VMEM((nch + 1, T_PAD, cw), F32), pltpu.VMEM((nch + 1, T_PAD, cw), F32),
                            pltpu.VMEM((N_KV_HEADS, nch + 1, rep * T_PAD, cw), F32),
                            pltpu.SemaphoreType.DMA((2,)), pltpu.SemaphoreType.DMA((2,))],
        ),
        out_shape=jax.ShapeDtypeStruct((b, N_KV_HEADS, rep * T_PAD, HEAD_DIM), F32),
        compiler_params=_params("arbitrary"),
        name="dsa_sample",
    )(page_table, q_s, iq_s, iw_s, k_new, v_new, ik_new,
      cache_idx_k, cache_k.reshape(n_phys, PAGE_SIZE, kvc), cache_v.reshape(n_phys, PAGE_SIZE, kvc))
    o = o.reshape(b, N_KV_HEADS, rep, T_PAD, HEAD_DIM)[:, :, :, :t]
    return o.transpose(0, 3, 1, 2, 4).reshape(b, t, N_HEADS * HEAD_DIM).astype(BF16)


Q_COLS = N_HEADS * HEAD_DIM
KV_COLS = N_KV_HEADS * HEAD_DIM
IQ_COLS = N_IDX_HEADS * IDX_DIM
IN_COLS = Q_COLS + 2 * KV_COLS + IQ_COLS + IDX_DIM + N_IDX_HEADS
PROMPT_TILE = 512
S5_TILE = 256


def _split_attn(proj, b, t):
    o = 0
    out = []
    for width, shape in ((Q_COLS, (N_HEADS, HEAD_DIM)), (KV_COLS, (N_KV_HEADS, HEAD_DIM)),
                         (KV_COLS, (N_KV_HEADS, HEAD_DIM)), (IQ_COLS, (N_IDX_HEADS, IDX_DIM)),
                         (IDX_DIM, (IDX_DIM,)), (N_IDX_HEADS, (N_IDX_HEADS,))):
        out.append(proj[:, o:o + width].reshape((b, t) + shape))
        o += width
    return out


def kernel(x_prompt, x_sample, cache_k, cache_v, cache_idx_k, state_ssm_re, state_ssm_im, state_ffn_conv,
           page_table, w_attn_in, w_attn_out, ssm_a_re, ssm_a_im, ssm_log_dt, ssm_b_re, ssm_b_im, ssm_c_re,
           ssm_c_im, ssm_d, w_glu, b_glu, norm_mixer, norm_ffn, w_ffn_up, ffn_conv_w, ffn_conv_b, w_ffn_down,
           norm_final):
    bp, sp, d = x_prompt.shape
    bs, ss, _ = x_sample.shape
    tp, ts = bp * sp, bs * ss
    xp = x_prompt.reshape(tp, d)
    xs = x_sample.reshape(ts, d)

    n_in = w_attn_in.shape[1]
    n_in_pad = -(-n_in // LANE) * LANE
    w_in = jnp.pad(w_attn_in, ((0, 0), (0, n_in_pad - n_in))).astype(BF16)
    w_out = w_attn_out.astype(BF16)
    q_p, k_p, v_p, iq_p, ik_p, iw_p = _split_attn(norm_matmul(xp, norm_mixer[0], w_in, PROMPT_TILE), bp, sp)
    q_s, k_s, v_s, iq_s, ik_s, iw_s = _split_attn(norm_matmul(xs, norm_mixer[0], w_in, ts), bs, ss)
    o_p = dsa_prompt_attention(q_p, k_p, v_p, iq_p, ik_p, iw_p)
    o_s = dsa_sample_attention(q_s, k_s, v_s, iq_s, ik_s, iw_s, cache_k, cache_v, cache_idx_k, page_table)
    xp = matmul_residual(o_p.reshape(tp, Q_COLS), w_out, xp, PROMPT_TILE)
    xs = matmul_residual(o_s.reshape(ts, Q_COLS), w_out, xs, ts)

    tiles_per_seq = sp // PROMPT_TILE
    xp, gt_p0 = conv_ffn(xp, norm_ffn[0], w_ffn_up[0], ffn_conv_w[0], ffn_conv_b[0], w_ffn_down[0],
                         tm=PROMPT_TILE, seq_len=sp)
    xs, gt_s0 = conv_ffn(xs, norm_ffn[0], w_ffn_up[0], ffn_conv_w[0], ffn_conv_b[0], w_ffn_down[0],
                         tm=ts, seq_len=ss, halo=state_ffn_conv[0])

    disc = _s5_discretize(ssm_a_re, ssm_a_im, ssm_log_dt, ssm_b_re, ssm_b_im)
    xp3, sre_p, sim_p = s5_layer(xp.reshape(bp, sp, d), norm_mixer[1], disc, ssm_c_re, ssm_c_im, ssm_d,
                                 w_glu, b_glu, tm=S5_TILE)
    xs3, sre_s, sim_s = s5_layer(xs.reshape(bs, ss, d), norm_mixer[1], disc, ssm_c_re, ssm_c_im, ssm_d,
                                 w_glu, b_glu, s0=(state_ssm_re, state_ssm_im))

    yp, gt_p1 = conv_ffn(xp3.reshape(tp, d), norm_ffn[1], w_ffn_up[1], ffn_conv_w[1], ffn_conv_b[1],
                         w_ffn_down[1], tm=PROMPT_TILE, seq_len=sp, g_final=norm_final)
    ys, gt_s1 = conv_ffn(xs3.reshape(ts, d), norm_ffn[1], w_ffn_up[1], ffn_conv_w[1], ffn_conv_b[1],
                         w_ffn_down[1], tm=ts, seq_len=ss, halo=state_ffn_conv[1], g_final=norm_final)

    f = w_ffn_down.shape[1]
    keep = CONV_W - 1

    def conv_state_p(gt):
        return gt.reshape(bp, tiles_per_seq, SUBLANE, f)[:, -1, SUBLANE - keep:]

    def conv_state_s(gt):
        return gt.reshape(bs, ss, f)[:, ss - keep:]

    conv_prompt = jnp.stack([conv_state_p(gt_p0), conv_state_p(gt_p1)])
    conv_sample = jnp.stack([conv_state_s(gt_s0), conv_state_s(gt_s1)])
    return (yp.reshape(bp, sp, d), ys.reshape(bs, ss, d), k_p, v_p, ik_p, k_s, v_s, ik_s,
            sre_p, sim_p, sre_s, sim_s, conv_prompt, conv_sample)
```

```python
import functools
import math

import jax
import jax.numpy as jnp
from jax import lax
from jax.experimental import pallas as pl
from jax.experimental.pallas import tpu as pltpu

F32 = jnp.float32
BF16 = jnp.bfloat16

N_HEADS = 8
HEAD_DIM = 128
N_KV_HEADS = 2
N_IDX_HEADS = 4
IDX_DIM = 64
TOPK_MAX = 256
Q_BLOCK = 128
PAGE_SIZE = 128
SSM_GROUP = 16
SSM_STATE = 64
CONV_W = 3
NORM_EPS = 1e-6
NEG_INF = -1e30

LANE = 128
SUBLANE = 8
VMEM_LIMIT_BYTES = 56 * 1024 * 1024


def _params(*sem):
    return pltpu.CompilerParams(dimension_semantics=sem, vmem_limit_bytes=VMEM_LIMIT_BYTES)


def _rmsnorm(x, g):
    return x * lax.rsqrt(jnp.mean(x * x, axis=-1, keepdims=True) + NORM_EPS) * g


def _const_spec(shape):
    nd = len(shape)
    return pl.BlockSpec(shape, lambda *_: (0,) * nd)


def _norm_matmul_kernel(x_ref, g_ref, w_ref, o_ref):
    h = _rmsnorm(x_ref[...], g_ref[...])
    o_ref[...] = jnp.dot(h.astype(BF16), w_ref[...], preferred_element_type=F32)


def norm_matmul(x, g, w, tm):
    t, d = x.shape
    n = w.shape[1]
    return pl.pallas_call(
        _norm_matmul_kernel,
        grid=(t // tm,),
        in_specs=[pl.BlockSpec((tm, d), lambda i: (i, 0)), _const_spec((1, d)), _const_spec((d, n))],
        out_specs=pl.BlockSpec((tm, n), lambda i: (i, 0)),
        out_shape=jax.ShapeDtypeStruct((t, n), F32),
        compiler_params=_params("parallel"),
        name="norm_matmul",
    )(x, g.reshape(1, d), w)


def _matmul_residual_kernel(a_ref, w_ref, x_ref, o_ref):
    o_ref[...] = x_ref[...] + jnp.dot(a_ref[...], w_ref[...], preferred_element_type=F32)


def matmul_residual(a, w, x, tm):
    t, k = a.shape
    n = w.shape[1]
    return pl.pallas_call(
        _matmul_residual_kernel,
        grid=(t // tm,),
        in_specs=[pl.BlockSpec((tm, k), lambda i: (i, 0)), _const_spec((k, n)),
                  pl.BlockSpec((tm, n), lambda i: (i, 0))],
        out_specs=pl.BlockSpec((tm, n), lambda i: (i, 0)),
        out_shape=jax.ShapeDtypeStruct((t, n), F32),
        compiler_params=_params("parallel"),
        name="matmul_residual",
    )(a, w, x)


def _ffn_kernel(*refs, tm, nc, seq_tiles, seq_len, halo, final_norm):
    refs = list(refs)
    x_ref, gn_ref, wg_ref, wu_ref, cw_ref, cb_ref, wd_ref = refs[:7]
    pos = 7
    if halo:
        h1_ref, h2_ref = refs[pos:pos + 2]
        pos += 2
    if final_norm:
        gf_ref = refs[pos]
        pos += 1
    y_ref, gt_ref = refs[pos:pos + 2]
    h_scr, gbuf, acc = refs[pos + 2:pos + 5]
    if not halo:
        carry = refs[pos + 5]

    x = x_ref[...]
    h_scr[...] = _rmsnorm(x, gn_ref[...]).astype(BF16)
    acc[...] = jnp.zeros_like(acc)
    if halo:
        t_in_seq = lax.broadcasted_iota(jnp.int32, (tm, 1), 0) % seq_len
    else:
        first = (pl.program_id(0) % seq_tiles) == 0

    def chunk(c, _):
        hb = h_scr[...]
        g = jnp.dot(hb, wg_ref[c], preferred_element_type=F32)
        u = jnp.dot(hb, wu_ref[c], preferred_element_type=F32)
        gbuf[pl.ds(SUBLANE, tm), :] = g
        if halo:
            gbuf[pl.ds(0, SUBLANE), :] = jnp.zeros((SUBLANE, g.shape[1]), F32)
            gm1 = jnp.where(t_in_seq >= 1, gbuf[pl.ds(SUBLANE - 1, tm), :], h1_ref[c])
            gm2 = jnp.where(t_in_seq >= 2, gbuf[pl.ds(SUBLANE - 2, tm), :], h2_ref[c])
            gt_ref[0, c] = g
        else:
            gbuf[pl.ds(0, SUBLANE), :] = jnp.where(first, 0.0, carry[c])
            gm1 = gbuf[pl.ds(SUBLANE - 1, tm), :]
            gm2 = gbuf[pl.ds(SUBLANE - 2, tm), :]
            tail = g[tm - SUBLANE:, :]
            carry[c] = tail
            gt_ref[0, c] = tail
        cw = cw_ref[c]
        gc = cw[0:1, :] * gm2 + cw[1:2, :] * gm1 + cw[2:3, :] * g + cb_ref[c]
        act = gc * jax.nn.sigmoid(gc) * u
        acc[...] += jnp.dot(act.astype(BF16), wd_ref[c], preferred_element_type=F32)
        return 0

    lax.fori_loop(0, nc, chunk, 0)
    y = x + acc[...]
    if final_norm:
        y = _rmsnorm(y, gf_ref[...])
    y_ref[...] = y


def conv_ffn(x, gn, w_up, conv_w, conv_b, w_down, *, tm, seq_len, halo=None, g_final=None, fc=256):
    t, d = x.shape
    f = w_down.shape[0]
    nc = f // fc
    wg = w_up[:, :f].astype(BF16).reshape(d, nc, fc).transpose(1, 0, 2)
    wu = w_up[:, f:].astype(BF16).reshape(d, nc, fc).transpose(1, 0, 2)
    wd = w_down.astype(BF16).reshape(nc, fc, d)
    cw = jnp.pad(conv_w, ((0, SUBLANE - CONV_W), (0, 0))).reshape(SUBLANE, nc, fc).transpose(1, 0, 2)
    cb = conv_b.reshape(nc, 1, fc)
    n_tiles = t // tm
    args = [x, gn.reshape(1, d), wg, wu, cw, cb, wd]
    in_specs = [pl.BlockSpec((tm, d), lambda i: (i, 0)), _const_spec((1, d)),
                _const_spec((nc, d, fc)), _const_spec((nc, d, fc)), _const_spec((nc, SUBLANE, fc)),
                _const_spec((nc, 1, fc)), _const_spec((nc, fc, d))]
    scratch = [pltpu.VMEM((tm, d), BF16), pltpu.VMEM((tm + SUBLANE, fc), F32), pltpu.VMEM((tm, d), F32)]
    if halo is not None:
        assert n_tiles == 1 and seq_len >= CONV_W - 1
        n_seq = t // seq_len
        z = jnp.zeros((n_seq, seq_len, f), F32)
        h1 = z.at[:, 0].set(halo[:, 1]).reshape(t, nc, fc).transpose(1, 0, 2)
        h2 = z.at[:, 0].set(halo[:, 0]).at[:, 1].set(halo[:, 1]).reshape(t, nc, fc).transpose(1, 0, 2)
        args += [h1, h2]
        in_specs += [_const_spec((nc, tm, fc)), _const_spec((nc, tm, fc))]
        tail_rows = tm
        seq_tiles = 1
    else:
        assert seq_len % tm == 0
        tail_rows = SUBLANE
        seq_tiles = seq_len // tm
        scratch.append(pltpu.VMEM((nc, SUBLANE, fc), F32))
    if g_final is not None:
        args.append(g_final.reshape(1, d))
        in_specs.append(_const_spec((1, d)))
    kern = functools.partial(_ffn_kernel, tm=tm, nc=nc, seq_tiles=seq_tiles, seq_len=seq_len,
                             halo=halo is not None, final_norm=g_final is not None)
    y, gt = pl.pallas_call(
        kern,
        grid=(n_tiles,),
        in_specs=in_specs,
        out_specs=[pl.BlockSpec((tm, d), lambda i: (i, 0)),
                   pl.BlockSpec((1, nc, tail_rows, fc), lambda i: (i, 0, 0, 0))],
        out_shape=[jax.ShapeDtypeStruct((t, d), F32),
                   jax.ShapeDtypeStruct((n_tiles, nc, tail_rows, fc), F32)],
        scratch_shapes=scratch,
        compiler_params=_params("arbitrary"),
        name="conv_ffn",
    )(*args)
    gt = gt.transpose(0, 2, 1, 3).reshape(n_tiles, tail_rows, f)
    return y, gt


S5_LANE_CHUNK = 512
S5_IN_BLOCK = 256
S5_OUT_GROUPS = 8


def _s5_kernel(*refs, tm, nseq, nsteps, chained):
    refs = list(refs)
    (x_ref, gn_ref, bb_ref, lam_ref, pw_ref, cre_ref, cim_ref, d_ref, wglu_ref, bglu_ref) = refs[:10]
    pos = 10
    if not chained:
        s0_ref = refs[pos]
        pos += 1
    y_ref, sfin_ref = refs[pos:pos + 2]
    xp, hp, sre, sim, yacc = refs[pos + 2:pos + 7]
    if chained:
        cin, st_carry = refs[pos + 7:pos + 9]
    n_state = sre.shape[1]
    d_model = x_ref.shape[-1]

    xp[...] = x_ref[0]
    h = _rmsnorm(xp[...], gn_ref[...])
    hp[...] = h.astype(BF16)

    n_in_blocks = d_model // S5_IN_BLOCK
    wcols = n_state // n_in_blocks
    for blk in range(n_in_blocks):
        bu = jnp.dot(hp[:, blk * S5_IN_BLOCK:(blk + 1) * S5_IN_BLOCK], bb_ref[blk],
                     preferred_element_type=F32)
        sre[:, blk * wcols:(blk + 1) * wcols] = bu[:, :wcols]
        sim[:, blk * wcols:(blk + 1) * wcols] = bu[:, wcols:]

    if chained:
        @pl.when(pl.program_id(1) == 0)
        def _():
            st_carry[...] = jnp.zeros_like(st_carry)

    lc = S5_LANE_CHUNK
    for c0 in range(0, n_state, lc):
        cols = slice(c0, c0 + lc)
        lr = jnp.broadcast_to(lam_ref[0:1, cols], (nseq, lc))
        li = jnp.broadcast_to(lam_ref[1:2, cols], (nseq, lc))
        if chained:
            init = (jnp.zeros((nseq, lc), F32), jnp.zeros((nseq, lc), F32))
        else:
            init = (s0_ref[0, :, cols], s0_ref[1, :, cols])

        def step(i, carry, cols=cols, lr=lr, li=li):
            cr, ci = carry
            rows = pl.ds(pl.multiple_of(i * nseq, nseq), nseq)
            nr = lr * cr - li * ci + sre[rows, cols]
            ni = lr * ci + li * cr + sim[rows, cols]
            sre[rows, cols] = nr
            sim[rows, cols] = ni
            return nr, ni

        cr, ci = lax.fori_loop(0, nsteps, step, init, unroll=4)

        if not chained:
            sfin_ref[0, :, cols] = cr
            sfin_ref[1, :, cols] = ci
        else:
            last = (nsteps - 1) * nseq
            pr_l = pw_ref[0, nsteps - 1:nsteps, cols]
            pi_l = pw_ref[1, nsteps - 1:nsteps, cols]
            c_r = st_carry[0:1, cols]
            c_i = st_carry[1:2, cols]
            for seq in range(nseq):
                cin[0, seq:seq + 1, :] = c_r
                cin[1, seq:seq + 1, :] = c_i
                e_r = sre[last + seq:last + seq + 1, cols]
                e_i = sim[last + seq:last + seq + 1, cols]
                c_r, c_i = pr_l * c_r - pi_l * c_i + e_r, pr_l * c_i + pi_l * c_r + e_i
            st_carry[0:1, cols] = c_r
            st_carry[1:2, cols] = c_i
            in_r = cin[0]
            in_i = cin[1]

            def fix(i, _, cols=cols, in_r=in_r, in_i=in_i):
                rows = pl.ds(pl.multiple_of(i * nseq, nseq), nseq)
                pr = jnp.broadcast_to(pw_ref[0, pl.ds(i, 1), cols], (nseq, lc))
                pi = jnp.broadcast_to(pw_ref[1, pl.ds(i, 1), cols], (nseq, lc))
                sre[rows, cols] = sre[rows, cols] + (pr * in_r - pi * in_i)
                sim[rows, cols] = sim[rows, cols] + (pr * in_i + pi * in_r)
                return 0

            lax.fori_loop(0, nsteps, fix, 0, unroll=4)

    if chained:
        sfin_ref[0] = st_carry[...]

    kcols = S5_OUT_GROUPS * SSM_STATE
    ncols = S5_OUT_GROUPS * SSM_GROUP
    for k in range(n_state // kcols):
        yk = jnp.dot(sre[:, k * kcols:(k + 1) * kcols].astype(BF16), cre_ref[k], preferred_element_type=F32)
        yk += jnp.dot(sim[:, k * kcols:(k + 1) * kcols].astype(BF16), cim_ref[k], preferred_element_type=F32)
        yacc[:, k * ncols:(k + 1) * ncols] = yk
    y = yacc[...] + d_ref[...] * _rmsnorm(xp[...], gn_ref[...])
    z = jnp.dot(jax.nn.gelu(y).astype(BF16), wglu_ref[...], preferred_element_type=F32) + bglu_ref[...]
    out = xp[...] + z[:, :d_model] * jax.nn.sigmoid(z[:, d_model:])
    y_ref[0] = out


def _s5_discretize(a_re, a_im, log_dt, b_re, b_im):
    a_re, a_im = a_re.astype(F32), a_im.astype(F32)
    dt = jnp.exp(log_dt.astype(F32))[:, None]
    mag = jnp.exp(a_re * dt)
    lam_re, lam_im = mag * jnp.cos(a_im * dt), mag * jnp.sin(a_im * dt)
    den = a_re * a_re + a_im * a_im
    n_re, n_im = lam_re - 1.0, lam_im
    f_re = (n_re * a_re + n_im * a_im) / den
    f_im = (n_im * a_re - n_re * a_im) / den
    b_re, b_im = b_re.astype(F32), b_im.astype(F32)
    bb_re = f_re[..., None] * b_re - f_im[..., None] * b_im
    bb_im = f_re[..., None] * b_im + f_im[..., None] * b_re
    return lam_re, lam_im, bb_re, bb_im


def _s5_weights(lam_re, lam_im, bb_re, bb_im, c_re, c_im, nsteps):
    n_groups = lam_re.shape[0]
    n_state = n_groups * SSM_STATE
    pr, pi = [lam_re.reshape(-1)], [lam_im.reshape(-1)]
    for _ in range(nsteps - 1):
        pr, pi = pr + [pr[-1] * pr[0] - pi[-1] * pi[0]], pi + [pr[-1] * pi[0] + pi[-1] * pr[0]]
    pw = jnp.stack([jnp.stack(pr), jnp.stack(pi)])
    lam = jnp.stack([lam_re.reshape(-1), lam_im.reshape(-1)])
    gpb = S5_IN_BLOCK // SSM_GROUP
    nb = n_groups // gpb
    eye = jnp.eye(gpb, dtype=F32)

    def bdiag_in(bb):
        bb = bb.reshape(nb, gpb, SSM_STATE, SSM_GROUP)
        return jnp.einsum('ngpc,gh->ngchp', bb, eye).reshape(nb, gpb * SSM_GROUP, gpb * SSM_STATE)

    bmat = jnp.concatenate([bdiag_in(bb_re), bdiag_in(bb_im)], axis=-1).astype(BF16)
    go = S5_OUT_GROUPS
    eye_o = jnp.eye(go, dtype=F32)

    def bdiag_out(cc):
        cc = cc.astype(F32).reshape(n_groups // go, go, SSM_GROUP, SSM_STATE)
        return jnp.einsum('ngcp,gh->ngphc', cc, eye_o).reshape(n_groups // go, go * SSM_STATE, go * SSM_GROUP)

    return lam, pw, bmat, bdiag_out(c_re).astype(BF16), bdiag_out(-c_im.astype(F32)).astype(BF16)


def s5_layer(x, gn, disc, c_re, c_im, d_skip, w_glu, b_glu, *, tm=None, s0=None):
    b, s, d = x.shape
    lam_re, lam_im, bb_re, bb_im = disc
    n_groups = lam_re.shape[0]
    n_state = n_groups * SSM_STATE
    chained = s0 is None
    if chained:
        nseq, nsteps = SUBLANE, tm // SUBLANE
        grid = (b, s // tm)
        xin = x.reshape(b, s // tm, nseq, nsteps, d).transpose(0, 1, 3, 2, 4).reshape(b, s, d)
        x_spec = pl.BlockSpec((1, tm, d), lambda i, j: (i, j, 0))
        sfin_spec = pl.BlockSpec((1, 2, n_state), lambda i, j: (i, 0, 0))
        sfin_shape = jax.ShapeDtypeStruct((b, 2, n_state), F32)
    else:
        nseq, nsteps, tm = b, s, b * s
        grid = (1, 1)
        xin = x.transpose(1, 0, 2).reshape(1, tm, d)
        x_spec = pl.BlockSpec((1, tm, d), lambda i, j: (0, 0, 0))
        sfin_spec = _const_spec((2, nseq, n_state))
        sfin_shape = jax.ShapeDtypeStruct((2, nseq, n_state), F32)
    lam, pw, bmat, cre, cim = _s5_weights(lam_re, lam_im, bb_re, bb_im, c_re, c_im, nsteps)
    args = [xin, gn.reshape(1, d), bmat, lam, pw, cre, cim, d_skip.reshape(1, d).astype(F32),
            w_glu.astype(BF16), b_glu.reshape(1, 2 * d).astype(F32)]
    in_specs = [x_spec, _const_spec((1, d)), _const_spec(bmat.shape), _const_spec(lam.shape),
                _const_spec(pw.shape), _const_spec(cre.shape), _const_spec(cim.shape),
                _const_spec((1, d)), _const_spec((d, 2 * d)), _const_spec((1, 2 * d))]
    scratch = [pltpu.VMEM((tm, d), F32), pltpu.VMEM((tm, d), BF16), pltpu.VMEM((tm, n_state), F32),
               pltpu.VMEM((tm, n_state), F32), pltpu.VMEM((tm, d), F32)]
    if chained:
        scratch += [pltpu.VMEM((2, nseq, S5_LANE_CHUNK), F32), pltpu.VMEM((2, n_state), F32)]
    else:
        args.append(jnp.stack([s0[0].reshape(b, n_state), s0[1].reshape(b, n_state)]).astype(F32))
        in_specs.append(_const_spec((2, nseq, n_state)))
    kern = functools.partial(_s5_kernel, tm=tm, nseq=nseq, nsteps=nsteps, chained=chained)
    y, sfin = pl.pallas_call(
        kern,
        grid=grid,
        in_specs=in_specs,
        out_specs=[x_spec, sfin_spec],
        out_shape=[jax.ShapeDtypeStruct(xin.shape, F32), sfin_shape],
        scratch_shapes=scratch,
        compiler_params=_params("arbitrary", "arbitrary"),
        name="s5_layer",
    )(*args)
    if chained:
        y = y.reshape(b, s // tm, nsteps, nseq, d).transpose(0, 1, 3, 2, 4).reshape(b, s, d)
        return (y, sfin[:, 0].reshape(b, n_groups, SSM_STATE), sfin[:, 1].reshape(b, n_groups, SSM_STATE))
    y = y.reshape(s, b, d).transpose(1, 0, 2)
    return y, sfin[0].reshape(b, n_groups, SSM_STATE), sfin[1].reshape(b, n_groups, SSM_STATE)


INT_MIN = -2 ** 31
N_KEY_BITS = 32


def _candidate(prefix, i):
    cand = prefix | lax.shift_left(jnp.int32(1), 31 - i)
    u = cand ^ jnp.int32(INT_MIN)
    bits = jnp.where(u >= 0, u, u ^ jnp.int32(0x7FFFFFFF))
    return cand, lax.bitcast_convert_type(bits, F32)


def _key_to_float(prefix):
    u = prefix ^ jnp.int32(INT_MIN)
    return lax.bitcast_convert_type(jnp.where(u >= 0, u, u ^ jnp.int32(0x7FFFFFFF)), F32)


KEY_CHUNK = 512
ATTN_KEY_STEP = KEY_CHUNK // 2
ATTN_SLAB = 64
QK_ROWS = 256
HEADS_PER_DOT = 2
LOG2_E = 1.4426950408889634


def _dsa_prompt_kernel(q_ref, k_ref, vt_ref, iq_ref, ik_ref, iw_ref, o_ref, sc, thr_scr, m_scr, l_scr, acc,
                       st_a, st_b, p_buf, *, topk):
    qb = Q_BLOCK
    kc = KEY_CHUNK
    hpd = HEADS_PER_DOT
    pairs_per_kv = N_HEADS // N_KV_HEADS // hpd
    j = pl.program_id(1)
    nkb = j + 1
    nkc = (nkb * qb + kc - 1) // kc

    def key_rows(kb):
        return pl.ds(pl.multiple_of(kb * qb, qb), qb)

    def chunk_rows(c):
        return pl.ds(pl.multiple_of(c * kc, kc), kc)

    kpos = lax.broadcasted_iota(jnp.int32, (kc, qb), 0)
    qpos = j * qb + lax.broadcasted_iota(jnp.int32, (kc, qb), 1)

    @pl.when(nkb * qb <= topk)
    def _():
        sc[chunk_rows(0), :] = jnp.where(kpos <= qpos, 0.0, NEG_INF)
        thr_scr[...] = jnp.zeros_like(thr_scr)

    @pl.when(nkb * qb > topk)
    def _():
        iq = iq_ref[0].reshape(N_IDX_HEADS * qb, IDX_DIM)
        iw = iw_ref[0]

        def score_chunk(c, _):
            dots = lax.dot_general(ik_ref[0, chunk_rows(c), :], iq, (((1,), (1,)), ((), ())),
                                   preferred_element_type=F32)
            s = jnp.zeros((kc, qb), F32)
            for h in range(N_IDX_HEADS):
                s = s + jnp.maximum(dots[:, h * qb:(h + 1) * qb], 0.0) * iw[h:h + 1, :]
            sc[chunk_rows(c), :] = jnp.where(kpos + c * kc <= qpos, s, NEG_INF)
            return 0

        lax.fori_loop(0, nkc, score_chunk, 0)

        kf = jnp.float32(topk)

        def count(pred):
            def blk(c, cnt):
                x = jnp.where(pred(sc[chunk_rows(c), :]), 1.0, 0.0).reshape(kc // SUBLANE, SUBLANE, qb)
                while x.shape[0] > 1:
                    half = x.shape[0] // 2
                    x = x[:half] + x[half:]
                return cnt + x[0]

            cnt = lax.fori_loop(0, nkc, blk, jnp.zeros((SUBLANE, qb), F32))
            return jnp.sum(cnt, axis=0, keepdims=True)

        def bit_body(i, prefix):
            cand, cf = _candidate(prefix, i)
            return jnp.where(count(lambda s: s >= cf) >= kf, cand, prefix)

        thr = _key_to_float(lax.fori_loop(0, N_KEY_BITS, bit_body, jnp.zeros((1, qb), jnp.int32)))
        thr_scr[...] = thr

        @pl.when(jnp.max(count(lambda s: s >= thr)) > kf)
        def _():
            need = kf - count(lambda s: s > thr)
            ki = lax.broadcasted_iota(jnp.int32, (qb, qb), 0)
            kj = lax.broadcasted_iota(jnp.int32, (qb, qb), 1)
            tri = jnp.where(kj <= ki, 1.0, 0.0).astype(BF16)

            def drop_blk(kb, run):
                s = sc[key_rows(kb), :]
                tie = s == thr
                pre = jnp.dot(tri, jnp.where(tie, 1.0, 0.0).astype(BF16), preferred_element_type=F32)
                sc[key_rows(kb), :] = jnp.where(tie & (pre + run > need), NEG_INF, s)
                return run + pre[qb - 1:qb, :]

            lax.fori_loop(0, nkb, drop_blk, jnp.zeros((1, qb), F32))

    m_scr[...] = jnp.full(m_scr.shape, NEG_INF, F32)
    l_scr[...] = jnp.zeros_like(l_scr)
    acc[...] = jnp.zeros_like(acc)
    thr = thr_scr[...]
    ks = ATTN_KEY_STEP
    slab = ATTN_SLAB
    cols = hpd * qb
    n_keys_total = sc.shape[0]

    def step_rows(key0):
        return pl.ds(pl.multiple_of(key0, ks), ks)

    def qk(key0, st_buf):
        for s0 in range(0, ks, QK_ROWS):
            rows = pl.ds(pl.multiple_of(key0 + s0, QK_ROWS), QK_ROWS)
            b1 = jnp.where(sc[rows, :] >= thr, 0.0, NEG_INF)
            bt = jnp.concatenate([b1] * hpd, axis=1)
            for g in range(N_KV_HEADS):
                kg = k_ref[0, g, rows, :]
                for pr in range(pairs_per_kv):
                    hp = g * pairs_per_kv + pr
                    qp = q_ref[0, hp * hpd:(hp + 1) * hpd].reshape(hpd * qb, HEAD_DIM)
                    st_buf[hp, s0:s0 + QK_ROWS, :] = lax.dot_general(
                        kg, qp, (((1,), (1,)), ((), ())), preferred_element_type=F32) + bt

    def slabs(ref, hp):
        for s0 in range(0, ks, slab):
            yield s0, ref[hp, s0:s0 + slab, :]

    def fold(x, op):
        x = x.reshape(slab // SUBLANE, SUBLANE, cols)
        return op(x, axis=0)

    def softmax_pv(key0, st_buf):
        for g in range(N_KV_HEADS):
            vtg = vt_ref[0, g, key0 // ks]
            for pr in range(pairs_per_kv):
                hp = g * pairs_per_kv + pr
                m_old = m_scr[hp]
                m8 = None
                for _, st in slabs(st_buf, hp):
                    f = fold(st, jnp.max)
                    m8 = f if m8 is None else jnp.maximum(m8, f)
                m_new = jnp.maximum(m_old, jnp.max(m8, axis=0, keepdims=True))
                alpha = jnp.exp2(m_old - m_new)
                l8 = jnp.zeros((SUBLANE, cols), F32)
                for s0, st in slabs(st_buf, hp):
                    p = jnp.exp2(st - m_new)
                    l8 = l8 + fold(p, jnp.sum)
                    p_buf[hp, s0:s0 + slab, :] = p.astype(BF16)
                l_scr[hp] = alpha * l_scr[hp] + jnp.sum(l8, axis=0, keepdims=True)
                acc[hp] = alpha * acc[hp] + jnp.dot(vtg, p_buf[hp], preferred_element_type=F32)
                m_scr[hp] = m_new

    qk(0, st_a)

    def attn_chunk(c, _):
        key0 = c * kc
        qk(key0 + ks, st_b)
        softmax_pv(key0, st_a)
        qk(jnp.minimum(key0 + 2 * ks, n_keys_total - ks), st_a)
        softmax_pv(key0 + ks, st_b)
        return 0

    lax.fori_loop(0, nkc, attn_chunk, 0)
    for hp in range(N_HEADS // hpd):
        ot = acc[hp] / l_scr[hp]
        for i in range(hpd):
            hd = hp * hpd + i
            o_ref[0, :, hd * HEAD_DIM:(hd + 1) * HEAD_DIM] = ot[:, i * qb:(i + 1) * qb].T.astype(o_ref.dtype)


def dsa_prompt_attention(q, k, v, iq, ik, iw):
    b, s = q.shape[:2]
    topk = min(TOPK_MAX, s // 4)
    assert topk % Q_BLOCK == 0 and topk <= KEY_CHUNK and s % KEY_CHUNK == 0
    nkb = s // Q_BLOCK
    q_hm = (q * (HEAD_DIM ** -0.5 * LOG2_E)).astype(BF16).transpose(0, 2, 1, 3)
    k_hm = k.astype(BF16).transpose(0, 2, 1, 3)
    n_steps = s // ATTN_KEY_STEP
    vt = v.astype(BF16).reshape(b, n_steps, ATTN_KEY_STEP, N_KV_HEADS, HEAD_DIM).transpose(0, 3, 1, 4, 2)
    iq_hm = iq.astype(BF16).transpose(0, 2, 1, 3)
    iw_t = (iw * (IDX_DIM ** -0.5 * N_IDX_HEADS ** -0.5)).transpose(0, 2, 1)
    n_dots = N_HEADS // HEADS_PER_DOT
    dot_cols = HEADS_PER_DOT * Q_BLOCK
    return pl.pallas_call(
        functools.partial(_dsa_prompt_kernel, topk=topk),
        grid=(b, nkb),
        in_specs=[
            pl.BlockSpec((1, N_HEADS, Q_BLOCK, HEAD_DIM), lambda i, j: (i, 0, j, 0)),
            pl.BlockSpec((1, N_KV_HEADS, s, HEAD_DIM), lambda i, j: (i, 0, 0, 0)),
            pl.BlockSpec((1, N_KV_HEADS, n_steps, HEAD_DIM, ATTN_KEY_STEP), lambda i, j: (i, 0, 0, 0, 0)),
            pl.BlockSpec((1, N_IDX_HEADS, Q_BLOCK, IDX_DIM), lambda i, j: (i, 0, j, 0)),
            pl.BlockSpec((1, s, IDX_DIM), lambda i, j: (i, 0, 0)),
            pl.BlockSpec((1, N_IDX_HEADS, Q_BLOCK), lambda i, j: (i, 0, j)),
        ],
        out_specs=pl.BlockSpec((1, Q_BLOCK, N_HEADS * HEAD_DIM), lambda i, j: (i, j, 0)),
        out_shape=jax.ShapeDtypeStruct((b, s, N_HEADS * HEAD_DIM), BF16),
        scratch_shapes=[pltpu.VMEM((s, Q_BLOCK), F32), pltpu.VMEM((1, Q_BLOCK), F32),
                        pltpu.VMEM((n_dots, 1, dot_cols), F32),
                        pltpu.VMEM((n_dots, 1, dot_cols), F32),
                        pltpu.VMEM((n_dots, HEAD_DIM, dot_cols), F32),
                        pltpu.VMEM((n_dots, ATTN_KEY_STEP, dot_cols), F32),
                        pltpu.VMEM((n_dots, ATTN_KEY_STEP, dot_cols), F32),
                        pltpu.VMEM((n_dots, ATTN_KEY_STEP, dot_cols), BF16)],
        compiler_params=_params("arbitrary", "arbitrary"),
        name="dsa_prompt",
    )(q_hm, k_hm, vt, iq_hm, ik.astype(BF16), iw_t)


PAGES_PER_CHUNK = 16
T_PAD = SUBLANE


def _dsa_sample_kernel(pt_ref, q_ref, iq_ref, iw_ref, knew_ref, vnew_ref, iknew_ref,
                       cik_hbm, ck_hbm, cv_hbm, o_ref,
                       ikbuf, kvbuf, sc, bias, lg, sem_ik, sem_kv, *, n_pages, n_new, topk):
    b = pl.program_id(0)
    cpp = PAGES_PER_CHUNK
    nch = n_pages // cpp
    cw = cpp * PAGE_SIZE
    rep = N_HEADS // N_KV_HEADS
    rows = rep * T_PAD

    def page_copy(src_hbm, buf, sem, chunk, slot, i):
        page = pt_ref[b, chunk * cpp + i]
        if buf is ikbuf:
            dst = buf.at[slot, :, pl.ds(i * PAGE_SIZE, PAGE_SIZE)]
        else:
            dst = buf.at[slot, pl.ds(i * PAGE_SIZE * N_KV_HEADS, PAGE_SIZE * N_KV_HEADS), :]
        return pltpu.make_async_copy(src_hbm.at[page], dst, sem.at[slot])

    def kv_rows(slot, g):
        return kvbuf[slot, pl.ds(g, cw, stride=N_KV_HEADS), :]

    def fetch(src_hbm, buf, sem, chunk, slot):
        for i in range(cpp):
            page_copy(src_hbm, buf, sem, chunk, slot, i).start()

    def wait(src_hbm, buf, sem, chunk, slot):
        for i in range(cpp):
            page_copy(src_hbm, buf, sem, chunk, slot, i).wait()

    def sweep(src_hbm, buf, sem, body):
        fetch(src_hbm, buf, sem, 0, 0)

        def step(c, _):
            slot = c % 2

            @pl.when(c + 1 < nch)
            def _():
                fetch(src_hbm, buf, sem, c + 1, 1 - slot)

            wait(src_hbm, buf, sem, c, slot)
            body(c, slot)
            return 0

        lax.fori_loop(0, nch, step, 0)

    iq = iq_ref[0]
    iw = iw_ref[0]

    def scores_of(ikc_t):
        dots = jnp.dot(iq, ikc_t, preferred_element_type=F32)
        s = jnp.zeros((T_PAD, ikc_t.shape[1]), F32)
        for h in range(N_IDX_HEADS):
            w = iw[h * T_PAD:(h + 1) * T_PAD, 0:1]
            s = s + jnp.maximum(dots[h * T_PAD:(h + 1) * T_PAD, :], 0.0) * w
        return s

    def score_chunk(c, slot):
        sc[c] = scores_of(ikbuf[slot].astype(BF16))

    sweep(cik_hbm, ikbuf, sem_ik, score_chunk)
    kcol = lax.broadcasted_iota(jnp.int32, (T_PAD, PAGE_SIZE), 1)
    trow = lax.broadcasted_iota(jnp.int32, (T_PAD, PAGE_SIZE), 0)
    new_ok = (kcol <= trow) & (kcol < n_new)
    sc[nch] = jnp.full((T_PAD, cw), NEG_INF, F32)
    sc[nch, :, 0:PAGE_SIZE] = jnp.where(new_ok, scores_of(iknew_ref[0]), NEG_INF)

    kf = jnp.float32(topk)

    def count(pred):
        def blk(c, cnt):
            return cnt + jnp.where(pred(sc[c]), 1.0, 0.0)

        cnt = lax.fori_loop(0, nch + 1, blk, jnp.zeros((T_PAD, cw), F32))
        return jnp.sum(cnt, axis=1, keepdims=True)

    def bit_body(i, prefix):
        cand, cf = _candidate(prefix, i)
        return jnp.where(count(lambda s: s >= cf) >= kf, cand, prefix)

    thr = _key_to_float(lax.fori_loop(0, N_KEY_BITS, bit_body, jnp.zeros((T_PAD, 1), jnp.int32)))
    need = kf - count(lambda s: s > thr)
    ki = lax.broadcasted_iota(jnp.int32, (LANE, LANE), 0)
    kj = lax.broadcasted_iota(jnp.int32, (LANE, LANE), 1)
    tri = jnp.where(ki <= kj, 1.0, 0.0).astype(BF16)

    def select_blk(c, run):
        s = sc[c]
        for t in range(cw // LANE):
            st = s[:, t * LANE:(t + 1) * LANE]
            tie = st == thr
            pre = jnp.dot(jnp.where(tie, 1.0, 0.0).astype(BF16), tri, preferred_element_type=F32)
            sel = (st > thr) | (tie & (pre + run <= need))
            bias[c, :, t * LANE:(t + 1) * LANE] = jnp.where(sel, 0.0, NEG_INF)
            run = run + pre[:, LANE - 1:LANE]
        return run

    lax.fori_loop(0, nch + 1, select_blk, jnp.zeros((T_PAD, 1), F32))

    def logits_of(kg, g, bias_c):
        qg = q_ref[0, g]
        lgt = lax.dot_general(qg, kg, (((1,), (1,)), ((), ())), preferred_element_type=F32)
        return lgt + jnp.concatenate([bias_c] * rep, axis=0)

    def logit_chunk(c, slot):
        for g in range(N_KV_HEADS):
            lg[g, c] = logits_of(kv_rows(slot, g).astype(BF16), g, bias[c])

    sweep(ck_hbm, kvbuf, sem_kv, logit_chunk)
    for g in range(N_KV_HEADS):
        lg[g, nch] = jnp.full((rows, cw), NEG_INF, F32)
        lg[g, nch, :, 0:PAGE_SIZE] = logits_of(knew_ref[0][:, g * HEAD_DIM:(g + 1) * HEAD_DIM], g,
                                               bias[nch, :, 0:PAGE_SIZE])

    denom = []
    for g in range(N_KV_HEADS):
        mx = lax.fori_loop(0, nch + 1, lambda c, m, g=g: jnp.maximum(m, lg[g, c]),
                           jnp.full((rows, cw), NEG_INF, F32))
        mx = jnp.max(mx, axis=1, keepdims=True)

        def exp_blk(c, tot, g=g, mx=mx):
            p = jnp.exp(lg[g, c] - mx)
            lg[g, c] = p
            return tot + p

        tot = lax.fori_loop(0, nch + 1, exp_blk, jnp.zeros((rows, cw), F32))
        denom.append(jnp.sum(tot, axis=1, keepdims=True))
        o_ref[0, g] = jnp.dot(lg[g, nch, :, 0:PAGE_SIZE].astype(BF16),
                              vnew_ref[0][:, g * HEAD_DIM:(g + 1) * HEAD_DIM], preferred_element_type=F32)

    def value_chunk(c, slot):
        for g in range(N_KV_HEADS):
            o_ref[0, g] += jnp.dot(lg[g, c].astype(BF16), kv_rows(slot, g).astype(BF16),
                                   preferred_element_type=F32)

    sweep(cv_hbm, kvbuf, sem_kv, value_chunk)
    for g in range(N_KV_HEADS):
        o_ref[0, g] = o_ref[0, g] / denom[g]


def dsa_sample_attention(q, k, v, iq, ik, iw, cache_k, cache_v, cache_idx_k, page_table):
    b, t = q.shape[:2]
    n_pages = page_table.shape[1]
    n_phys = cache_k.shape[0]
    n_keys = n_pages * PAGE_SIZE + t
    topk = min(TOPK_MAX, n_keys // 4)
    rep = N_HEADS // N_KV_HEADS
    assert t <= T_PAD and n_pages % PAGES_PER_CHUNK == 0 and n_pages * PAGE_SIZE >= topk
    nch = n_pages // PAGES_PER_CHUNK
    cw = PAGES_PER_CHUNK * PAGE_SIZE

    def pad_t(x):
        return jnp.pad(x, ((0, 0), (0, T_PAD - t)) + ((0, 0),) * (x.ndim - 2))

    def pad_page(x):
        return jnp.pad(x, ((0, 0), (0, PAGE_SIZE - t), (0, 0)))

    kvc = N_KV_HEADS * HEAD_DIM
    q_s = pad_t((q * HEAD_DIM ** -0.5).astype(BF16)).reshape(b, T_PAD, N_KV_HEADS, rep, HEAD_DIM)
    q_s = q_s.transpose(0, 2, 3, 1, 4).reshape(b, N_KV_HEADS, rep * T_PAD, HEAD_DIM)
    iq_s = pad_t(iq.astype(BF16)).transpose(0, 2, 1, 3).reshape(b, N_IDX_HEADS * T_PAD, IDX_DIM)
    iw_s = pad_t(iw * (IDX_DIM ** -0.5 * N_IDX_HEADS ** -0.5)).transpose(0, 2, 1)
    iw_s = jnp.broadcast_to(iw_s.reshape(b, N_IDX_HEADS * T_PAD, 1), (b, N_IDX_HEADS * T_PAD, LANE))
    k_new = pad_page(k.reshape(b, t, kvc).astype(BF16))
    v_new = pad_page(v.reshape(b, t, kvc).astype(BF16))
    ik_new = pad_page(ik.astype(BF16)).transpose(0, 2, 1)

    def bspec(shape):
        nd = len(shape)
        return pl.BlockSpec((1,) + shape, lambda i, pt: (i,) + (0,) * nd)

    kern = functools.partial(_dsa_sample_kernel, n_pages=n_pages, n_new=t, topk=topk)
    o = pl.pallas_call(
        kern,
        grid_spec=pltpu.PrefetchScalarGridSpec(
            num_scalar_prefetch=1,
            grid=(b,),
            in_specs=[bspec((N_KV_HEADS, rep * T_PAD, HEAD_DIM)), bspec((N_IDX_HEADS * T_PAD, IDX_DIM)),
                      bspec((N_IDX_HEADS * T_PAD, LANE)), bspec((PAGE_SIZE, kvc)), bspec((PAGE_SIZE, kvc)),
                      bspec((IDX_DIM, PAGE_SIZE)),
                      pl.BlockSpec(memory_space=pl.ANY), pl.BlockSpec(memory_space=pl.ANY),
                      pl.BlockSpec(memory_space=pl.ANY)],
            out_specs=bspec((N_KV_HEADS, rep * T_PAD, HEAD_DIM)),
            scratch_shapes=[pltpu.VMEM((2, IDX_DIM, cw), F32), pltpu.VMEM((2, cw * N_KV_HEADS, HEAD_DIM), F32),
                            pltpu.VMEM((nch + 1, T_PAD, cw), F32), pltpu.VMEM((nch + 1, T_PAD, cw), F32),
                            pltpu.VMEM((N_KV_HEADS, nch + 1, rep * T_PAD, cw), F32),
                            pltpu.SemaphoreType.DMA((2,)), pltpu.SemaphoreType.DMA((2,))],
        ),
        out_shape=jax.ShapeDtypeStruct((b, N_KV_HEADS, rep * T_PAD, HEAD_DIM), F32),
        compiler_params=_params("arbitrary"),
        name="dsa_sample",
    )(page_table, q_s, iq_s, iw_s, k_new, v_new, ik_new,
      cache_idx_k.transpose(0, 2, 1),
      cache_k.reshape(n_phys, PAGE_SIZE * N_KV_HEADS, HEAD_DIM),
      cache_v.reshape(n_phys, PAGE_SIZE * N_KV_HEADS, HEAD_DIM))
    o = o.reshape(b, N_KV_HEADS, rep, T_PAD, HEAD_DIM)[:, :, :, :t]
    return o.transpose(0, 3, 1, 2, 4).reshape(b, t, N_HEADS * HEAD_DIM).astype(BF16)


Q_COLS = N_HEADS * HEAD_DIM
KV_COLS = N_KV_HEADS * HEAD_DIM
IQ_COLS = N_IDX_HEADS * IDX_DIM
IN_COLS = Q_COLS + 2 * KV_COLS + IQ_COLS + IDX_DIM + N_IDX_HEADS
PROMPT_TILE = 512
S5_TILE = 256


def _split_attn(proj, b, t):
    o = 0
    out = []
    for width, shape in ((Q_COLS, (N_HEADS, HEAD_DIM)), (KV_COLS, (N_KV_HEADS, HEAD_DIM)),
                         (KV_COLS, (N_KV_HEADS, HEAD_DIM)), (IQ_COLS, (N_IDX_HEADS, IDX_DIM)),
                         (IDX_DIM, (IDX_DIM,)), (N_IDX_HEADS, (N_IDX_HEADS,))):
        out.append(proj[:, o:o + width].reshape((b, t) + shape))
        o += width
    return out


def kernel(x_prompt, x_sample, cache_k, cache_v, cache_idx_k, state_ssm_re, state_ssm_im, state_ffn_conv,
           page_table, w_attn_in, w_attn_out, ssm_a_re, ssm_a_im, ssm_log_dt, ssm_b_re, ssm_b_im, ssm_c_re,
           ssm_c_im, ssm_d, w_glu, b_glu, norm_mixer, norm_ffn, w_ffn_up, ffn_conv_w, ffn_conv_b, w_ffn_down,
           norm_final):
    bp, sp, d = x_prompt.shape
    bs, ss, _ = x_sample.shape
    tp, ts = bp * sp, bs * ss
    xp = x_prompt.reshape(tp, d)
    xs = x_sample.reshape(ts, d)

    n_in = w_attn_in.shape[1]
    n_in_pad = -(-n_in // LANE) * LANE
    w_in = jnp.pad(w_attn_in, ((0, 0), (0, n_in_pad - n_in))).astype(BF16)
    w_out = w_attn_out.astype(BF16)
    q_p, k_p, v_p, iq_p, ik_p, iw_p = _split_attn(norm_matmul(xp, norm_mixer[0], w_in, PROMPT_TILE), bp, sp)
    q_s, k_s, v_s, iq_s, ik_s, iw_s = _split_attn(norm_matmul(xs, norm_mixer[0], w_in, ts), bs, ss)
    o_p = dsa_prompt_attention(q_p, k_p, v_p, iq_p, ik_p, iw_p)
    o_s = dsa_sample_attention(q_s, k_s, v_s, iq_s, ik_s, iw_s, cache_k, cache_v, cache_idx_k, page_table)
    xp = matmul_residual(o_p.reshape(tp, Q_COLS), w_out, xp, PROMPT_TILE)
    xs = matmul_residual(o_s.reshape(ts, Q_COLS), w_out, xs, ts)

    tiles_per_seq = sp // PROMPT_TILE
    xp, gt_p0 = conv_ffn(xp, norm_ffn[0], w_ffn_up[0], ffn_conv_w[0], ffn_conv_b[0], w_ffn_down[0],
                         tm=PROMPT_TILE, seq_len=sp)
    xs, gt_s0 = conv_ffn(xs, norm_ffn[0], w_ffn_up[0], ffn_conv_w[0], ffn_conv_b[0], w_ffn_down[0],
                         tm=ts, seq_len=ss, halo=state_ffn_conv[0])

    disc = _s5_discretize(ssm_a_re, ssm_a_im, ssm_log_dt, ssm_b_re, ssm_b_im)
    xp3, sre_p, sim_p = s5_layer(xp.reshape(bp, sp, d), norm_mixer[1], disc, ssm_c_re, ssm_c_im, ssm_d,
                                 w_glu, b_glu, tm=S5_TILE)
    xs3, sre_s, sim_s = s5_layer(xs.reshape(bs, ss, d), norm_mixer[1], disc, ssm_c_re, ssm_c_im, ssm_d,
                                 w_glu, b_glu, s0=(state_ssm_re, state_ssm_im))

    yp, gt_p1 = conv_ffn(xp3.reshape(tp, d), norm_ffn[1], w_ffn_up[1], ffn_conv_w[1], ffn_conv_b[1],
                         w_ffn_down[1], tm=PROMPT_TILE, seq_len=sp, g_final=norm_final)
    ys, gt_s1 = conv_ffn(xs3.reshape(ts, d), norm_ffn[1], w_ffn_up[1], ffn_conv_w[1], ffn_conv_b[1],
                         w_ffn_down[1], tm=ts, seq_len=ss, halo=state_ffn_conv[1], g_final=norm_final)

    f = w_ffn_down.shape[1]
    keep = CONV_W - 1

    def conv_state_p(gt):
        return gt.reshape(bp, tiles_per_seq, SUBLANE, f)[:, -1, SUBLANE - keep:]

    def conv_state_s(gt):
        return gt.reshape(bs, ss, f)[:, ss - keep:]

    conv_prompt = jnp.stack([conv_state_p(gt_p0), conv_state_p(gt_p1)])
    conv_sample = jnp.stack([conv_state_s(gt_s0), conv_state_s(gt_s1)])
    return (yp.reshape(bp, sp, d), ys.reshape(bs, ss, d), k_p, v_p, ik_p, k_s, v_s, ik_s,
            sre_p, sim_p, sre_s, sim_s, conv_prompt, conv_sample)
```

```python
import functools
import math

import jax
import jax.numpy as jnp
from jax import lax
from jax.experimental import pallas as pl
from jax.experimental.pallas import tpu as pltpu

F32 = jnp.float32
BF16 = jnp.bfloat16

N_HEADS = 8
HEAD_DIM = 128
N_KV_HEADS = 2
N_IDX_HEADS = 4
IDX_DIM = 64
TOPK_MAX = 256
Q_BLOCK = 128
PAGE_SIZE = 128
SSM_GROUP = 16
SSM_STATE = 64
CONV_W = 3
NORM_EPS = 1e-6
NEG_INF = -1e30

LANE = 128
SUBLANE = 8
VMEM_LIMIT_BYTES = 56 * 1024 * 1024


def _params(*sem):
    return pltpu.CompilerParams(dimension_semantics=sem, vmem_limit_bytes=VMEM_LIMIT_BYTES)


def _rmsnorm(x, g):
    return x * lax.rsqrt(jnp.mean(x * x, axis=-1, keepdims=True) + NORM_EPS) * g


def _const_spec(shape):
    nd = len(shape)
    return pl.BlockSpec(shape, lambda *_: (0,) * nd)


def _norm_matmul_kernel(x_ref, g_ref, w_ref, o_ref):
    h = _rmsnorm(x_ref[...], g_ref[...])
    o_ref[...] = jnp.dot(h.astype(BF16), w_ref[...], preferred_element_type=F32)


def norm_matmul(x, g, w, tm):
    t, d = x.shape
    n = w.shape[1]
    return pl.pallas_call(
        _norm_matmul_kernel,
        grid=(t // tm,),
        in_specs=[pl.BlockSpec((tm, d), lambda i: (i, 0)), _const_spec((1, d)), _const_spec((d, n))],
        out_specs=pl.BlockSpec((tm, n), lambda i: (i, 0)),
        out_shape=jax.ShapeDtypeStruct((t, n), F32),
        compiler_params=_params("parallel"),
        name="norm_matmul",
    )(x, g.reshape(1, d), w)


def _matmul_residual_kernel(a_ref, w_ref, x_ref, o_ref):
    o_ref[...] = x_ref[...] + jnp.dot(a_ref[...], w_ref[...], preferred_element_type=F32)


def matmul_residual(a, w, x, tm):
    t, k = a.shape
    n = w.shape[1]
    return pl.pallas_call(
        _matmul_residual_kernel,
        grid=(t // tm,),
        in_specs=[pl.BlockSpec((tm, k), lambda i: (i, 0)), _const_spec((k, n)),
                  pl.BlockSpec((tm, n), lambda i: (i, 0))],
        out_specs=pl.BlockSpec((tm, n), lambda i: (i, 0)),
        out_shape=jax.ShapeDtypeStruct((t, n), F32),
        compiler_params=_params("parallel"),
        name="matmul_residual",
    )(a, w, x)


FFN_ACT_SLABS = 4


def _ffn_kernel(*refs, tm, nc, seq_tiles, seq_len, halo, final_norm):
    refs = list(refs)
    x_ref, gn_ref, wg_ref, wu_ref, cw_ref, cb_ref, wd_ref = refs[:7]
    pos = 7
    if halo:
        h1_ref, h2_ref = refs[pos:pos + 2]
        pos += 2
    if final_norm:
        gf_ref = refs[pos]
        pos += 1
    y_ref, gt_ref = refs[pos:pos + 2]
    h_scr, acc, g_a, u_a, g_b, u_b, act_buf = refs[pos + 2:pos + 9]
    if not halo:
        carry = refs[pos + 9]

    x = x_ref[...]
    h_scr[...] = _rmsnorm(x, gn_ref[...]).astype(BF16)
    acc[...] = jnp.zeros_like(acc)
    if halo:
        t_in_seq = lax.broadcasted_iota(jnp.int32, (tm, 1), 0) % seq_len
    else:
        first = (pl.program_id(0) % seq_tiles) == 0

    def up_g(c, gbuf):
        gbuf[pl.ds(SUBLANE, tm), :] = jnp.dot(h_scr[...], wg_ref[c], preferred_element_type=F32)

    def up_u(c, ubuf):
        ubuf[...] = jnp.dot(h_scr[...], wu_ref[c], preferred_element_type=F32)

    n_slabs = FFN_ACT_SLABS if tm % (FFN_ACT_SLABS * SUBLANE) == 0 else 1
    rs = tm // n_slabs

    def halo_rows(c, gbuf):
        if halo:
            gbuf[pl.ds(0, SUBLANE), :] = jnp.zeros((SUBLANE, gbuf.shape[1]), F32)
        else:
            gbuf[pl.ds(0, SUBLANE), :] = jnp.where(first, 0.0, carry[c])
            tail = gbuf[pl.ds(tm, SUBLANE), :]
            carry[c] = tail
            gt_ref[0, c] = tail

    def act_rows(c, gbuf, ubuf, r):
        r0 = r * rs
        g = gbuf[pl.ds(SUBLANE + r0, rs), :]
        gm1 = gbuf[pl.ds(SUBLANE - 1 + r0, rs), :]
        gm2 = gbuf[pl.ds(SUBLANE - 2 + r0, rs), :]
        if halo:
            t = t_in_seq[r0:r0 + rs]
            gm1 = jnp.where(t >= 1, gm1, h1_ref[c, r0:r0 + rs, :])
            gm2 = jnp.where(t >= 2, gm2, h2_ref[c, r0:r0 + rs, :])
            gt_ref[0, c, r0:r0 + rs, :] = g
        cw = cw_ref[c]
        gc = cw[0:1, :] * gm2 + cw[1:2, :] * gm1 + cw[2:3, :] * g + cb_ref[c]
        act_buf[r0:r0 + rs, :] = (gc * jax.nn.sigmoid(gc) * ubuf[r0:r0 + rs, :]).astype(BF16)

    def down(c):
        acc[...] += jnp.dot(act_buf[...], wd_ref[c], preferred_element_type=F32)

    def half(c_act, gbuf, ubuf, c_up, gnext, unext):
        halo_rows(c_act, gbuf)
        if c_up is not None:
            up_g(c_up, gnext)
        for r in range(n_slabs):
            if c_up is not None and r == n_slabs // 2:
                up_u(c_up, unext)
            act_rows(c_act, gbuf, ubuf, r)
        down(c_act)

    up_g(0, g_a)
    up_u(0, u_a)

    def two_chunks(i, _):
        c = 2 * i
        half(c, g_a, u_a, c + 1, g_b, u_b)
        half(c + 1, g_b, u_b, jnp.minimum(c + 2, nc - 1), g_a, u_a)
        return 0

    lax.fori_loop(0, nc // 2, two_chunks, 0)
    if nc % 2:
        half(nc - 1, g_a, u_a, None, None, None)
    y = x + acc[...]
    if final_norm:
        y = _rmsnorm(y, gf_ref[...])
    y_ref[...] = y


def conv_ffn(x, gn, w_up, conv_w, conv_b, w_down, *, tm, seq_len, halo=None, g_final=None, fc=256):
    t, d = x.shape
    f = w_down.shape[0]
    nc = f // fc
    wg = w_up[:, :f].astype(BF16).reshape(d, nc, fc).transpose(1, 0, 2)
    wu = w_up[:, f:].astype(BF16).reshape(d, nc, fc).transpose(1, 0, 2)
    wd = w_down.astype(BF16).reshape(nc, fc, d)
    cw = jnp.pad(conv_w, ((0, SUBLANE - CONV_W), (0, 0))).reshape(SUBLANE, nc, fc).transpose(1, 0, 2)
    cb = conv_b.reshape(nc, 1, fc)
    n_tiles = t // tm
    args = [x, gn.reshape(1, d), wg, wu, cw, cb, wd]
    in_specs = [pl.BlockSpec((tm, d), lambda i: (i, 0)), _const_spec((1, d)),
                _const_spec((nc, d, fc)), _const_spec((nc, d, fc)), _const_spec((nc, SUBLANE, fc)),
                _const_spec((nc, 1, fc)), _const_spec((nc, fc, d))]
    scratch = [pltpu.VMEM((tm, d), BF16), pltpu.VMEM((tm, d), F32),
               pltpu.VMEM((tm + SUBLANE, fc), F32), pltpu.VMEM((tm, fc), F32),
               pltpu.VMEM((tm + SUBLANE, fc), F32), pltpu.VMEM((tm, fc), F32),
               pltpu.VMEM((tm, fc), BF16)]
    if halo is not None:
        assert n_tiles == 1 and seq_len >= CONV_W - 1
        n_seq = t // seq_len
        z = jnp.zeros((n_seq, seq_len, f), F32)
        h1 = z.at[:, 0].set(halo[:, 1]).reshape(t, nc, fc).transpose(1, 0, 2)
        h2 = z.at[:, 0].set(halo[:, 0]).at[:, 1].set(halo[:, 1]).reshape(t, nc, fc).transpose(1, 0, 2)
        args += [h1, h2]
        in_specs += [_const_spec((nc, tm, fc)), _const_spec((nc, tm, fc))]
        tail_rows = tm
        seq_tiles = 1
    else:
        assert seq_len % tm == 0
        tail_rows = SUBLANE
        seq_tiles = seq_len // tm
        scratch.append(pltpu.VMEM((nc, SUBLANE, fc), F32))
    if g_final is not None:
        args.append(g_final.reshape(1, d))
        in_specs.append(_const_spec((1, d)))
    kern = functools.partial(_ffn_kernel, tm=tm, nc=nc, seq_tiles=seq_tiles, seq_len=seq_len,
                             halo=halo is not None, final_norm=g_final is not None)
    y, gt = pl.pallas_call(
        kern,
        grid=(n_tiles,),
        in_specs=in_specs,
        out_specs=[pl.BlockSpec((tm, d), lambda i: (i, 0)),
                   pl.BlockSpec((1, nc, tail_rows, fc), lambda i: (i, 0, 0, 0))],
        out_shape=[jax.ShapeDtypeStruct((t, d), F32),
                   jax.ShapeDtypeStruct((n_tiles, nc, tail_rows, fc), F32)],
        scratch_shapes=scratch,
        compiler_params=_params("arbitrary"),
        name="conv_ffn",
    )(*args)
    gt = gt.transpose(0, 2, 1, 3).reshape(n_tiles, tail_rows, f)
    return y, gt


S5_LANE_CHUNK = 512
S5_IN_BLOCK = 256
S5_OUT_GROUPS = 8


def _s5_kernel(*refs, tm, nseq, nsteps, chained):
    refs = list(refs)
    (x_ref, gn_ref, bb_ref, lam_ref, pw_ref, cre_ref, cim_ref, d_ref, wglu_ref, bglu_ref) = refs[:10]
    pos = 10
    if not chained:
        s0_ref = refs[pos]
        pos += 1
    y_ref, sfin_ref = refs[pos:pos + 2]
    xp, hp, sre, sim, yacc = refs[pos + 2:pos + 7]
    if chained:
        cin, st_carry = refs[pos + 7:pos + 9]
    n_state = sre.shape[1]
    d_model = x_ref.shape[-1]

    xp[...] = x_ref[0]
    h = _rmsnorm(xp[...], gn_ref[...])
    hp[...] = h.astype(BF16)

    n_in_blocks = d_model // S5_IN_BLOCK
    wcols = n_state // n_in_blocks
    for blk in range(n_in_blocks):
        bu = jnp.dot(hp[:, blk * S5_IN_BLOCK:(blk + 1) * S5_IN_BLOCK], bb_ref[blk],
                     preferred_element_type=F32)
        sre[:, blk * wcols:(blk + 1) * wcols] = bu[:, :wcols]
        sim[:, blk * wcols:(blk + 1) * wcols] = bu[:, wcols:]

    if chained:
        @pl.when(pl.program_id(1) == 0)
        def _():
            st_carry[...] = jnp.zeros_like(st_carry)

    lc = S5_LANE_CHUNK
    for c0 in range(0, n_state, lc):
        cols = slice(c0, c0 + lc)
        lr = jnp.broadcast_to(lam_ref[0:1, cols], (nseq, lc))
        li = jnp.broadcast_to(lam_ref[1:2, cols], (nseq, lc))
        if chained:
            init = (jnp.zeros((nseq, lc), F32), jnp.zeros((nseq, lc), F32))
        else:
            init = (s0_ref[0, :, cols], s0_ref[1, :, cols])

        def step(i, carry, cols=cols, lr=lr, li=li):
            cr, ci = carry
            rows = pl.ds(pl.multiple_of(i * nseq, nseq), nseq)
            nr = lr * cr - li * ci + sre[rows, cols]
            ni = lr * ci + li * cr + sim[rows, cols]
            sre[rows, cols] = nr
            sim[rows, cols] = ni
            return nr, ni

        cr, ci = lax.fori_loop(0, nsteps, step, init, unroll=4)

        if not chained:
            sfin_ref[0, :, cols] = cr
            sfin_ref[1, :, cols] = ci
        else:
            last = (nsteps - 1) * nseq
            pr_l = pw_ref[0, nsteps - 1:nsteps, cols]
            pi_l = pw_ref[1, nsteps - 1:nsteps, cols]
            c_r = st_carry[0:1, cols]
            c_i = st_carry[1:2, cols]
            for seq in range(nseq):
                cin[0, seq:seq + 1, :] = c_r
                cin[1, seq:seq + 1, :] = c_i
                e_r = sre[last + seq:last + seq + 1, cols]
                e_i = sim[last + seq:last + seq + 1, cols]
                c_r, c_i = pr_l * c_r - pi_l * c_i + e_r, pr_l * c_i + pi_l * c_r + e_i
            st_carry[0:1, cols] = c_r
            st_carry[1:2, cols] = c_i
            in_r = cin[0]
            in_i = cin[1]

            def fix(i, _, cols=cols, in_r=in_r, in_i=in_i):
                rows = pl.ds(pl.multiple_of(i * nseq, nseq), nseq)
                pr = jnp.broadcast_to(pw_ref[0, pl.ds(i, 1), cols], (nseq, lc))
                pi = jnp.broadcast_to(pw_ref[1, pl.ds(i, 1), cols], (nseq, lc))
                sre[rows, cols] = sre[rows, cols] + (pr * in_r - pi * in_i)
                sim[rows, cols] = sim[rows, cols] + (pr * in_i + pi * in_r)
                return 0

            lax.fori_loop(0, nsteps, fix, 0, unroll=4)

    if chained:
        sfin_ref[0] = st_carry[...]

    kcols = S5_OUT_GROUPS * SSM_STATE
    ncols = S5_OUT_GROUPS * SSM_GROUP
    for k in range(n_state // kcols):
        yk = jnp.dot(sre[:, k * kcols:(k + 1) * kcols].astype(BF16), cre_ref[k], preferred_element_type=F32)
        yk += jnp.dot(sim[:, k * kcols:(k + 1) * kcols].astype(BF16), cim_ref[k], preferred_element_type=F32)
        yacc[:, k * ncols:(k + 1) * ncols] = yk
    y = yacc[...] + d_ref[...] * _rmsnorm(xp[...], gn_ref[...])
    z = jnp.dot(jax.nn.gelu(y).astype(BF16), wglu_ref[...], preferred_element_type=F32) + bglu_ref[...]
    out = xp[...] + z[:, :d_model] * jax.nn.sigmoid(z[:, d_model:])
    y_ref[0] = out


def _s5_discretize(a_re, a_im, log_dt, b_re, b_im):
    a_re, a_im = a_re.astype(F32), a_im.astype(F32)
    dt = jnp.exp(log_dt.astype(F32))[:, None]
    mag = jnp.exp(a_re * dt)
    lam_re, lam_im = mag * jnp.cos(a_im * dt), mag * jnp.sin(a_im * dt)
    den = a_re * a_re + a_im * a_im
    n_re, n_im = lam_re - 1.0, lam_im
    f_re = (n_re * a_re + n_im * a_im) / den
    f_im = (n_im * a_re - n_re * a_im) / den
    b_re, b_im = b_re.astype(F32), b_im.astype(F32)
    bb_re = f_re[..., None] * b_re - f_im[..., None] * b_im
    bb_im = f_re[..., None] * b_im + f_im[..., None] * b_re
    return lam_re, lam_im, bb_re, bb_im


def _s5_weights(lam_re, lam_im, bb_re, bb_im, c_re, c_im, nsteps):
    n_groups = lam_re.shape[0]
    n_state = n_groups * SSM_STATE
    pr, pi = [lam_re.reshape(-1)], [lam_im.reshape(-1)]
    for _ in range(nsteps - 1):
        pr, pi = pr + [pr[-1] * pr[0] - pi[-1] * pi[0]], pi + [pr[-1] * pi[0] + pi[-1] * pr[0]]
    pw = jnp.stack([jnp.stack(pr), jnp.stack(pi)])
    lam = jnp.stack([lam_re.reshape(-1), lam_im.reshape(-1)])
    gpb = S5_IN_BLOCK // SSM_GROUP
    nb = n_groups // gpb
    eye = jnp.eye(gpb, dtype=F32)

    def bdiag_in(bb):
        bb = bb.reshape(nb, gpb, SSM_STATE, SSM_GROUP)
        return jnp.einsum('ngpc,gh->ngchp', bb, eye).reshape(nb, gpb * SSM_GROUP, gpb * SSM_STATE)

    bmat = jnp.concatenate([bdiag_in(bb_re), bdiag_in(bb_im)], axis=-1).astype(BF16)
    go = S5_OUT_GROUPS
    eye_o = jnp.eye(go, dtype=F32)

    def bdiag_out(cc):
        cc = cc.astype(F32).reshape(n_groups // go, go, SSM_GROUP, SSM_STATE)
        return jnp.einsum('ngcp,gh->ngphc', cc, eye_o).reshape(n_groups // go, go * SSM_STATE, go * SSM_GROUP)

    return lam, pw, bmat, bdiag_out(c_re).astype(BF16), bdiag_out(-c_im.astype(F32)).astype(BF16)


def s5_layer(x, gn, disc, c_re, c_im, d_skip, w_glu, b_glu, *, tm=None, s0=None):
    b, s, d = x.shape
    lam_re, lam_im, bb_re, bb_im = disc
    n_groups = lam_re.shape[0]
    n_state = n_groups * SSM_STATE
    chained = s0 is None
    if chained:
        nseq, nsteps = SUBLANE, tm // SUBLANE
        grid = (b, s // tm)
        xin = x.reshape(b, s // tm, nseq, nsteps, d).transpose(0, 1, 3, 2, 4).reshape(b, s, d)
        x_spec = pl.BlockSpec((1, tm, d), lambda i, j: (i, j, 0))
        sfin_spec = pl.BlockSpec((1, 2, n_state), lambda i, j: (i, 0, 0))
        sfin_shape = jax.ShapeDtypeStruct((b, 2, n_state), F32)
    else:
        nseq, nsteps, tm = b, s, b * s
        grid = (1, 1)
        xin = x.transpose(1, 0, 2).reshape(1, tm, d)
        x_spec = pl.BlockSpec((1, tm, d), lambda i, j: (0, 0, 0))
        sfin_spec = _const_spec((2, nseq, n_state))
        sfin_shape = jax.ShapeDtypeStruct((2, nseq, n_state), F32)
    lam, pw, bmat, cre, cim = _s5_weights(lam_re, lam_im, bb_re, bb_im, c_re, c_im, nsteps)
    args = [xin, gn.reshape(1, d), bmat, lam, pw, cre, cim, d_skip.reshape(1, d).astype(F32),
            w_glu.astype(BF16), b_glu.reshape(1, 2 * d).astype(F32)]
    in_specs = [x_spec, _const_spec((1, d)), _const_spec(bmat.shape), _const_spec(lam.shape),
                _const_spec(pw.shape), _const_spec(cre.shape), _const_spec(cim.shape),
                _const_spec((1, d)), _const_spec((d, 2 * d)), _const_spec((1, 2 * d))]
    scratch = [pltpu.VMEM((tm, d), F32), pltpu.VMEM((tm, d), BF16), pltpu.VMEM((tm, n_state), F32),
               pltpu.VMEM((tm, n_state), F32), pltpu.VMEM((tm, d), F32)]
    if chained:
        scratch += [pltpu.VMEM((2, nseq, S5_LANE_CHUNK), F32), pltpu.VMEM((2, n_state), F32)]
    else:
        args.append(jnp.stack([s0[0].reshape(b, n_state), s0[1].reshape(b, n_state)]).astype(F32))
        in_specs.append(_const_spec((2, nseq, n_state)))
    kern = functools.partial(_s5_kernel, tm=tm, nseq=nseq, nsteps=nsteps, chained=chained)
    y, sfin = pl.pallas_call(
        kern,
        grid=grid,
        in_specs=in_specs,
        out_specs=[x_spec, sfin_spec],
        out_shape=[jax.ShapeDtypeStruct(xin.shape, F32), sfin_shape],
        scratch_shapes=scratch,
        compiler_params=_params("arbitrary", "arbitrary"),
        name="s5_layer",
    )(*args)
    if chained:
        y = y.reshape(b, s // tm, nsteps, nseq, d).transpose(0, 1, 3, 2, 4).reshape(b, s, d)
        return (y, sfin[:, 0].reshape(b, n_groups, SSM_STATE), sfin[:, 1].reshape(b, n_groups, SSM_STATE))
    y = y.reshape(s, b, d).transpose(1, 0, 2)
    return y, sfin[0].reshape(b, n_groups, SSM_STATE), sfin[1].reshape(b, n_groups, SSM_STATE)


INT_MIN = -2 ** 31
N_KEY_BITS = 32


def _candidate(prefix, i):
    cand = prefix | lax.shift_left(jnp.int32(1), 31 - i)
    u = cand ^ jnp.int32(INT_MIN)
    bits = jnp.where(u >= 0, u, u ^ jnp.int32(0x7FFFFFFF))
    return cand, lax.bitcast_convert_type(bits, F32)


def _key_to_float(prefix):
    u = prefix ^ jnp.int32(INT_MIN)
    return lax.bitcast_convert_type(jnp.where(u >= 0, u, u ^ jnp.int32(0x7FFFFFFF)), F32)


KEY_CHUNK = 512
ATTN_KEY_STEP = KEY_CHUNK // 2
ATTN_SLAB = 64
QK_ROWS = 256
COUNT_ROWS = 128
COUNT_ACCS = 4
HEADS_PER_DOT = 2
LOG2_E = 1.4426950408889634


def _dsa_prompt_kernel(q_ref, k_ref, vt_ref, iq_ref, ik_ref, iw_ref, o_ref, sc, thr_scr, m_scr, l_scr, acc,
                       st_a, st_b, p_buf, *, topk, nkc):
    qb = Q_BLOCK
    kc = KEY_CHUNK
    hpd = HEADS_PER_DOT
    pairs_per_kv = N_HEADS // N_KV_HEADS // hpd
    j = (nkc - 1) * (kc // qb) + pl.program_id(1)
    nkb = j + 1

    def key_rows(kb):
        return pl.ds(pl.multiple_of(kb * qb, qb), qb)

    def chunk_rows(c):
        return slice(c * kc, (c + 1) * kc)

    kpos = lax.broadcasted_iota(jnp.int32, (kc, qb), 0)
    qpos = j * qb + lax.broadcasted_iota(jnp.int32, (kc, qb), 1)

    @pl.when(nkb * qb <= topk)
    def _():
        sc[chunk_rows(0), :] = jnp.where(kpos <= qpos, 0.0, NEG_INF)
        thr_scr[...] = jnp.zeros_like(thr_scr)

    @pl.when(nkb * qb > topk)
    def _():
        iq = iq_ref[0].reshape(N_IDX_HEADS * qb, IDX_DIM)
        iw = iw_ref[0]

        for c in range(nkc):
            dots = lax.dot_general(ik_ref[0, chunk_rows(c), :], iq, (((1,), (1,)), ((), ())),
                                   preferred_element_type=F32)
            s = jnp.zeros((kc, qb), F32)
            for h in range(N_IDX_HEADS):
                s = s + jnp.maximum(dots[:, h * qb:(h + 1) * qb], 0.0) * iw[h:h + 1, :]
            sc[chunk_rows(c), :] = jnp.where(kpos + c * kc <= qpos, s, NEG_INF)

        kf = jnp.float32(topk)

        def count(pred):
            cnt = jnp.zeros((COUNT_ACCS, SUBLANE, qb), F32)
            for c in range(nkc):
                for r0 in range(0, kc, COUNT_ROWS):
                    x = jnp.where(pred(sc[c * kc + r0:c * kc + r0 + COUNT_ROWS, :]), 1.0, 0.0)
                    x = x.reshape(COUNT_ROWS // SUBLANE // COUNT_ACCS, COUNT_ACCS, SUBLANE, qb)
                    cnt = cnt + jnp.sum(x, axis=0)
            return jnp.sum(jnp.sum(cnt, axis=0), axis=0, keepdims=True)

        def bit_body(i, prefix):
            cand, cf = _candidate(prefix, i)
            return jnp.where(count(lambda s: s >= cf) >= kf, cand, prefix)

        thr = _key_to_float(lax.fori_loop(0, N_KEY_BITS, bit_body, jnp.zeros((1, qb), jnp.int32)))
        thr_scr[...] = thr

        @pl.when(jnp.max(count(lambda s: s >= thr)) > kf)
        def _():
            need = kf - count(lambda s: s > thr)
            ki = lax.broadcasted_iota(jnp.int32, (qb, qb), 0)
            kj = lax.broadcasted_iota(jnp.int32, (qb, qb), 1)
            tri = jnp.where(kj <= ki, 1.0, 0.0).astype(BF16)

            def drop_blk(kb, run):
                s = sc[key_rows(kb), :]
                tie = s == thr
                pre = jnp.dot(tri, jnp.where(tie, 1.0, 0.0).astype(BF16), preferred_element_type=F32)
                sc[key_rows(kb), :] = jnp.where(tie & (pre + run > need), NEG_INF, s)
                return run + pre[qb - 1:qb, :]

            lax.fori_loop(0, nkb, drop_blk, jnp.zeros((1, qb), F32))

    m_scr[...] = jnp.full(m_scr.shape, NEG_INF, F32)
    l_scr[...] = jnp.zeros_like(l_scr)
    acc[...] = jnp.zeros_like(acc)
    thr = thr_scr[...]
    ks = ATTN_KEY_STEP
    slab = ATTN_SLAB
    cols = hpd * qb

    def step_rows(key0):
        return slice(key0, key0 + ks)

    def qk(key0, st_buf):
        for s0 in range(0, ks, QK_ROWS):
            rows = slice(key0 + s0, key0 + s0 + QK_ROWS)
            b1 = jnp.where(sc[rows, :] >= thr, 0.0, NEG_INF)
            bt = jnp.concatenate([b1] * hpd, axis=1)
            for g in range(N_KV_HEADS):
                kg = k_ref[0, g, rows, :]
                for pr in range(pairs_per_kv):
                    hp = g * pairs_per_kv + pr
                    qp = q_ref[0, hp * hpd:(hp + 1) * hpd].reshape(hpd * qb, HEAD_DIM)
                    st_buf[hp, s0:s0 + QK_ROWS, :] = lax.dot_general(
                        kg, qp, (((1,), (1,)), ((), ())), preferred_element_type=F32) + bt

    def slabs(ref, hp):
        for s0 in range(0, ks, slab):
            yield s0, ref[hp, s0:s0 + slab, :]

    def fold(x, op):
        x = x.reshape(slab // SUBLANE, SUBLANE, cols)
        return op(x, axis=0)

    def softmax_pv(key0, st_buf):
        for g in range(N_KV_HEADS):
            vtg = vt_ref[0, g, key0 // ks]
            for pr in range(pairs_per_kv):
                hp = g * pairs_per_kv + pr
                m_old = m_scr[hp]
                m8 = None
                for _, st in slabs(st_buf, hp):
                    f = fold(st, jnp.max)
                    m8 = f if m8 is None else jnp.maximum(m8, f)
                m_new = jnp.maximum(m_old, jnp.max(m8, axis=0, keepdims=True))
                alpha = jnp.exp2(m_old - m_new)
                l8 = jnp.zeros((SUBLANE, cols), F32)
                for s0, st in slabs(st_buf, hp):
                    p = jnp.exp2(st - m_new)
                    l8 = l8 + fold(p, jnp.sum)
                    p_buf[hp, s0:s0 + slab, :] = p.astype(BF16)
                l_scr[hp] = alpha * l_scr[hp] + jnp.sum(l8, axis=0, keepdims=True)
                acc[hp] = alpha * acc[hp] + jnp.dot(vtg, p_buf[hp], preferred_element_type=F32)
                m_scr[hp] = m_new

    qk(0, st_a)
    for c in range(nkc):
        key0 = c * kc
        qk(key0 + ks, st_b)
        softmax_pv(key0, st_a)
        if c + 1 < nkc:
            qk(key0 + 2 * ks, st_a)
        softmax_pv(key0 + ks, st_b)
    for hp in range(N_HEADS // hpd):
        ot = acc[hp] / l_scr[hp]
        for i in range(hpd):
            hd = hp * hpd + i
            o_ref[0, :, hd * HEAD_DIM:(hd + 1) * HEAD_DIM] = ot[:, i * qb:(i + 1) * qb].T.astype(o_ref.dtype)


def dsa_prompt_attention(q, k, v, iq, ik, iw):
    b, s = q.shape[:2]
    topk = min(TOPK_MAX, s // 4)
    assert topk % Q_BLOCK == 0 and topk <= KEY_CHUNK and s % KEY_CHUNK == 0
    q_hm = (q * (HEAD_DIM ** -0.5 * LOG2_E)).astype(BF16).transpose(0, 2, 1, 3)
    k_hm = k.astype(BF16).transpose(0, 2, 1, 3)
    n_steps = s // ATTN_KEY_STEP
    vt = v.astype(BF16).reshape(b, n_steps, ATTN_KEY_STEP, N_KV_HEADS, HEAD_DIM).transpose(0, 3, 1, 4, 2)
    iq_hm = iq.astype(BF16).transpose(0, 2, 1, 3)
    iw_t = (iw * (IDX_DIM ** -0.5 * N_IDX_HEADS ** -0.5)).transpose(0, 2, 1)
    n_dots = N_HEADS // HEADS_PER_DOT
    dot_cols = HEADS_PER_DOT * Q_BLOCK
    args = (q_hm, k_hm, vt, iq_hm, ik.astype(BF16), iw_t)
    bpc = KEY_CHUNK // Q_BLOCK
    steps_per_chunk = KEY_CHUNK // ATTN_KEY_STEP
    return jnp.concatenate([_dsa_prompt_call(args, b, topk, nkc, bpc, steps_per_chunk, n_dots, dot_cols)
                            for nkc in range(1, s // KEY_CHUNK + 1)], axis=1)


def _dsa_prompt_call(args, b, topk, nkc, bpc, steps_per_chunk, n_dots, dot_cols):
    nk = nkc * KEY_CHUNK
    j0 = (nkc - 1) * bpc
    return pl.pallas_call(
        functools.partial(_dsa_prompt_kernel, topk=topk, nkc=nkc),
        grid=(b, bpc),
        in_specs=[
            pl.BlockSpec((1, N_HEADS, Q_BLOCK, HEAD_DIM), lambda i, j: (i, 0, j0 + j, 0)),
            pl.BlockSpec((1, N_KV_HEADS, nk, HEAD_DIM), lambda i, j: (i, 0, 0, 0)),
            pl.BlockSpec((1, N_KV_HEADS, nkc * steps_per_chunk, HEAD_DIM, ATTN_KEY_STEP),
                         lambda i, j: (i, 0, 0, 0, 0)),
            pl.BlockSpec((1, N_IDX_HEADS, Q_BLOCK, IDX_DIM), lambda i, j: (i, 0, j0 + j, 0)),
            pl.BlockSpec((1, nk, IDX_DIM), lambda i, j: (i, 0, 0)),
            pl.BlockSpec((1, N_IDX_HEADS, Q_BLOCK), lambda i, j: (i, 0, j0 + j)),
        ],
        out_specs=pl.BlockSpec((1, Q_BLOCK, N_HEADS * HEAD_DIM), lambda i, j: (i, j, 0)),
        out_shape=jax.ShapeDtypeStruct((b, bpc * Q_BLOCK, N_HEADS * HEAD_DIM), BF16),
        scratch_shapes=[pltpu.VMEM((nk, Q_BLOCK), F32), pltpu.VMEM((1, Q_BLOCK), F32),
                        pltpu.VMEM((n_dots, 1, dot_cols), F32),
                        pltpu.VMEM((n_dots, 1, dot_cols), F32),
                        pltpu.VMEM((n_dots, HEAD_DIM, dot_cols), F32),
                        pltpu.VMEM((n_dots, ATTN_KEY_STEP, dot_cols), F32),
                        pltpu.VMEM((n_dots, ATTN_KEY_STEP, dot_cols), F32),
                        pltpu.VMEM((n_dots, ATTN_KEY_STEP, dot_cols), BF16)],
        compiler_params=_params("arbitrary", "arbitrary"),
        name="dsa_prompt",
    )(*args)


PAGES_PER_CHUNK = 16
T_PAD = SUBLANE


RING_SLOTS = 4
SELECT_BATCH = 16


def _page_ring(pt_ref, streams, nch, buf, sem, dst_of_page, n_batches):
    cpp = PAGES_PER_CHUNK
    n = len(streams) * nch
    assert n % RING_SLOTS == 0
    b = pl.program_id(0)

    def copies(batch, i):
        src, c, slot = streams[i // nch], i % nch, i % RING_SLOTS
        return [pltpu.make_async_copy(src.at[pt_ref[batch, c * cpp + p]], dst_of_page(buf, slot, p),
                                      sem.at[slot]) for p in range(cpp)]

    def request(i):
        if i < n:
            for cp in copies(b, i):
                cp.start()
        else:
            @pl.when(b + 1 < n_batches)
            def _():
                for cp in copies(b + 1, i - n):
                    cp.start()

    def wait(i):
        for cp in copies(b, i):
            cp.wait()

    def prologue():
        @pl.when(b == 0)
        def _():
            for i in range(RING_SLOTS - 1):
                request(i)

    return request, wait, prologue


def _sample_scores_kernel(pt_ref, iq_ref, iw_ref, iknew_ref, cik_hbm, sc_ref, ikbuf, sem,
                          *, n_pages, n_new, n_batches):
    nch = n_pages // PAGES_PER_CHUNK
    cw = PAGES_PER_CHUNK * PAGE_SIZE
    request, wait, prologue = _page_ring(
        pt_ref, [cik_hbm], nch, ikbuf, sem,
        lambda buf, slot, p: buf.at[slot, :, pl.ds(p * PAGE_SIZE, PAGE_SIZE)], n_batches)
    iq = iq_ref[0]
    iw = iw_ref[0]

    def scores_of(ikc_t):
        dots = jnp.dot(iq, ikc_t, preferred_element_type=F32)
        s = jnp.zeros((T_PAD, ikc_t.shape[1]), F32)
        for h in range(N_IDX_HEADS):
            w = iw[h * T_PAD:(h + 1) * T_PAD, 0:1]
            s = s + jnp.maximum(dots[h * T_PAD:(h + 1) * T_PAD, :], 0.0) * w
        return s

    prologue()
    for i in range(nch):
        wait(i)
        request(i + RING_SLOTS - 1)
        sc_ref[0, i] = scores_of(ikbuf[i % RING_SLOTS].astype(BF16))
    kcol = lax.broadcasted_iota(jnp.int32, (T_PAD, PAGE_SIZE), 1)
    trow = lax.broadcasted_iota(jnp.int32, (T_PAD, PAGE_SIZE), 0)
    new_ok = (kcol <= trow) & (kcol < n_new)
    sc_ref[0, nch] = jnp.full((T_PAD, cw), NEG_INF, F32)
    sc_ref[0, nch, :, 0:PAGE_SIZE] = jnp.where(new_ok, scores_of(iknew_ref[0]), NEG_INF)


def _sample_select_kernel(sc_ref, sc2_ref, thr_ref, *, n_new, topk):
    sb, nblk, _, cw = sc_ref.shape
    kf = jnp.float32(topk)

    def lane_fold(x):
        parts = [x[:, :, t * LANE:(t + 1) * LANE] for t in range(cw // LANE)]
        while len(parts) > 1:
            parts = [parts[i] + parts[i + 1] for i in range(0, len(parts), 2)]
        return parts[0]

    def count(pred):
        def blk(c, cnt):
            return cnt + lane_fold(jnp.where(pred(sc_ref[:, c]), 1.0, 0.0))

        cnt = lax.fori_loop(0, nblk, blk, jnp.zeros((sb, T_PAD, LANE), F32))
        return jnp.sum(cnt, axis=2, keepdims=True)

    def bit_body(i, prefix):
        cand, cf = _candidate(prefix, i)
        return jnp.where(count(lambda s: s >= cf) >= kf, cand, prefix)

    thr = _key_to_float(lax.fori_loop(0, N_KEY_BITS, bit_body, jnp.zeros((sb, T_PAD, 1), jnp.int32)))
    thr_ref[...] = jnp.broadcast_to(thr, thr_ref.shape)

    def copy_blk(c, _):
        sc2_ref[:, c] = sc_ref[:, c]
        return 0

    lax.fori_loop(0, nblk, copy_blk, 0)
    real = lax.broadcasted_iota(jnp.int32, (sb, T_PAD, 1), 1) < n_new

    @pl.when(jnp.max(jnp.where(real, count(lambda s: s >= thr), 0.0)) > kf)
    def _():
        need = kf - count(lambda s: s > thr)
        ki = lax.broadcasted_iota(jnp.int32, (LANE, LANE), 0)
        kj = lax.broadcasted_iota(jnp.int32, (LANE, LANE), 1)
        tri = jnp.where(ki <= kj, 1.0, 0.0).astype(BF16)

        def drop_blk(c, run):
            s = sc_ref[:, c]
            for t in range(cw // LANE):
                st = s[:, :, t * LANE:(t + 1) * LANE]
                tie = st == thr
                ind = jnp.where(tie, 1.0, 0.0).reshape(sb * T_PAD, LANE).astype(BF16)
                pre = jnp.dot(ind, tri, preferred_element_type=F32).reshape(sb, T_PAD, LANE)
                sc2_ref[:, c, :, t * LANE:(t + 1) * LANE] = jnp.where(tie & (pre + run > need), NEG_INF, st)
                run = run + pre[:, :, LANE - 1:LANE]
            return run

        lax.fori_loop(0, nblk, drop_blk, jnp.zeros((sb, T_PAD, 1), F32))


def _sample_attend_kernel(pt_ref, q_ref, knew_ref, vnew_ref, sc_ref, thr_ref, ck_hbm, cv_hbm, o_ref,
                          kvbuf, lg, sem, *, n_pages, n_batches):
    nch = n_pages // PAGES_PER_CHUNK
    cw = PAGES_PER_CHUNK * PAGE_SIZE
    rep = N_HEADS // N_KV_HEADS
    rows = rep * T_PAD
    page_rows = PAGE_SIZE * N_KV_HEADS
    request, wait, prologue = _page_ring(
        pt_ref, [ck_hbm, cv_hbm], nch, kvbuf, sem,
        lambda buf, slot, p: buf.at[slot, pl.ds(p * page_rows, page_rows), :], n_batches)
    thr = thr_ref[0][:, 0:1]

    def kv_rows(slot, g):
        return kvbuf[slot, pl.ds(g, cw, stride=N_KV_HEADS), :].astype(BF16)

    def logits_of(kg, g, s):
        qg = q_ref[0, g]
        lgt = lax.dot_general(qg, kg, (((1,), (1,)), ((), ())), preferred_element_type=F32)
        return lgt + jnp.concatenate([jnp.where(s >= thr, 0.0, NEG_INF)] * rep, axis=0)

    prologue()
    for i in range(nch):
        wait(i)
        request(i + RING_SLOTS - 1)
        for g in range(N_KV_HEADS):
            lg[g, i] = logits_of(kv_rows(i % RING_SLOTS, g), g, sc_ref[0, i])
    for g in range(N_KV_HEADS):
        lg[g, nch] = jnp.full((rows, cw), NEG_INF, F32)
        lg[g, nch, :, 0:PAGE_SIZE] = logits_of(knew_ref[0][:, g * HEAD_DIM:(g + 1) * HEAD_DIM], g,
                                               sc_ref[0, nch, :, 0:PAGE_SIZE])

    denom = []
    for g in range(N_KV_HEADS):
        mx = lax.fori_loop(0, nch + 1, lambda c, m, g=g: jnp.maximum(m, lg[g, c]),
                           jnp.full((rows, cw), NEG_INF, F32))
        mx = jnp.max(mx, axis=1, keepdims=True)

        def exp_blk(c, tot, g=g, mx=mx):
            p = jnp.exp(lg[g, c] - mx)
            lg[g, c] = p
            return tot + p

        tot = lax.fori_loop(0, nch + 1, exp_blk, jnp.zeros((rows, cw), F32))
        denom.append(jnp.sum(tot, axis=1, keepdims=True))
        o_ref[0, g] = jnp.dot(lg[g, nch, :, 0:PAGE_SIZE].astype(BF16),
                              vnew_ref[0][:, g * HEAD_DIM:(g + 1) * HEAD_DIM], preferred_element_type=F32)

    for i in range(nch, 2 * nch):
        wait(i)
        request(i + RING_SLOTS - 1)
        for g in range(N_KV_HEADS):
            o_ref[0, g] += jnp.dot(lg[g, i - nch].astype(BF16), kv_rows(i % RING_SLOTS, g),
                                   preferred_element_type=F32)
    for g in range(N_KV_HEADS):
        o_ref[0, g] = o_ref[0, g] / denom[g]


def dsa_sample_attention(q, k, v, iq, ik, iw, cache_k, cache_v, cache_idx_k, page_table):
    b, t = q.shape[:2]
    n_pages = page_table.shape[1]
    n_phys = cache_k.shape[0]
    n_keys = n_pages * PAGE_SIZE + t
    topk = min(TOPK_MAX, n_keys // 4)
    rep = N_HEADS // N_KV_HEADS
    assert t <= T_PAD and n_pages % PAGES_PER_CHUNK == 0 and n_pages * PAGE_SIZE >= topk
    nch = n_pages // PAGES_PER_CHUNK
    cw = PAGES_PER_CHUNK * PAGE_SIZE

    def pad_t(x):
        return jnp.pad(x, ((0, 0), (0, T_PAD - t)) + ((0, 0),) * (x.ndim - 2))

    def pad_page(x):
        return jnp.pad(x, ((0, 0), (0, PAGE_SIZE - t), (0, 0)))

    kvc = N_KV_HEADS * HEAD_DIM
    q_s = pad_t((q * HEAD_DIM ** -0.5).astype(BF16)).reshape(b, T_PAD, N_KV_HEADS, rep, HEAD_DIM)
    q_s = q_s.transpose(0, 2, 3, 1, 4).reshape(b, N_KV_HEADS, rep * T_PAD, HEAD_DIM)
    iq_s = pad_t(iq.astype(BF16)).transpose(0, 2, 1, 3).reshape(b, N_IDX_HEADS * T_PAD, IDX_DIM)
    iw_s = pad_t(iw * (IDX_DIM ** -0.5 * N_IDX_HEADS ** -0.5)).transpose(0, 2, 1)
    iw_s = jnp.broadcast_to(iw_s.reshape(b, N_IDX_HEADS * T_PAD, 1), (b, N_IDX_HEADS * T_PAD, LANE))
    k_new = pad_page(k.reshape(b, t, kvc).astype(BF16))
    v_new = pad_page(v.reshape(b, t, kvc).astype(BF16))
    ik_new = pad_page(ik.astype(BF16)).transpose(0, 2, 1)

    def bspec(shape):
        nd = len(shape)
        return pl.BlockSpec((1,) + shape, lambda i, pt: (i,) + (0,) * nd)

    any_spec = pl.BlockSpec(memory_space=pl.ANY)
    sc_shape = (nch + 1, T_PAD, cw)
    sc = pl.pallas_call(
        functools.partial(_sample_scores_kernel, n_pages=n_pages, n_new=t, n_batches=b),
        grid_spec=pltpu.PrefetchScalarGridSpec(
            num_scalar_prefetch=1,
            grid=(b,),
            in_specs=[bspec((N_IDX_HEADS * T_PAD, IDX_DIM)), bspec((N_IDX_HEADS * T_PAD, LANE)),
                      bspec((IDX_DIM, PAGE_SIZE)), any_spec],
            out_specs=bspec(sc_shape),
            scratch_shapes=[pltpu.VMEM((RING_SLOTS, IDX_DIM, cw), F32), pltpu.SemaphoreType.DMA((RING_SLOTS,))],
        ),
        out_shape=jax.ShapeDtypeStruct((b,) + sc_shape, F32),
        compiler_params=_params("arbitrary"),
        name="dsa_sample_scores",
    )(page_table, iq_s, iw_s, ik_new, cache_idx_k.transpose(0, 2, 1))

    sb = math.gcd(b, SELECT_BATCH)
    sc_blk = pl.BlockSpec((sb,) + sc_shape, lambda i: (i, 0, 0, 0))
    sc, thr = pl.pallas_call(
        functools.partial(_sample_select_kernel, n_new=t, topk=topk),
        grid=(b // sb,),
        in_specs=[sc_blk],
        out_specs=[sc_blk, pl.BlockSpec((sb, T_PAD, LANE), lambda i: (i, 0, 0))],
        out_shape=[jax.ShapeDtypeStruct((b,) + sc_shape, F32), jax.ShapeDtypeStruct((b, T_PAD, LANE), F32)],
        compiler_params=_params("arbitrary"),
        name="dsa_sample_select",
    )(sc)

    o = pl.pallas_call(
        functools.partial(_sample_attend_kernel, n_pages=n_pages, n_batches=b),
        grid_spec=pltpu.PrefetchScalarGridSpec(
            num_scalar_prefetch=1,
            grid=(b,),
            in_specs=[bspec((N_KV_HEADS, rep * T_PAD, HEAD_DIM)), bspec((PAGE_SIZE, kvc)), bspec((PAGE_SIZE, kvc)),
                      bspec(sc_shape), bspec((T_PAD, LANE)), any_spec, any_spec],
            out_specs=bspec((N_KV_HEADS, rep * T_PAD, HEAD_DIM)),
            scratch_shapes=[pltpu.VMEM((RING_SLOTS, cw * N_KV_HEADS, HEAD_DIM), F32),
                            pltpu.VMEM((N_KV_HEADS, nch + 1, rep * T_PAD, cw), F32),
                            pltpu.SemaphoreType.DMA((RING_SLOTS,))],
        ),
        out_shape=jax.ShapeDtypeStruct((b, N_KV_HEADS, rep * T_PAD, HEAD_DIM), F32),
        compiler_params=_params("arbitrary"),
        name="dsa_sample_attend",
    )(page_table, q_s, k_new, v_new, sc, thr,
      cache_k.reshape(n_phys, PAGE_SIZE * N_KV_HEADS, HEAD_DIM),
      cache_v.reshape(n_phys, PAGE_SIZE * N_KV_HEADS, HEAD_DIM))
    o = o.reshape(b, N_KV_HEADS, rep, T_PAD, HEAD_DIM)[:, :, :, :t]
    return o.transpose(0, 3, 1, 2, 4).reshape(b, t, N_HEADS * HEAD_DIM).astype(BF16)


Q_COLS = N_HEADS * HEAD_DIM
KV_COLS = N_KV_HEADS * HEAD_DIM
IQ_COLS = N_IDX_HEADS * IDX_DIM
IN_COLS = Q_COLS + 2 * KV_COLS + IQ_COLS + IDX_DIM + N_IDX_HEADS
PROMPT_TILE = 512
S5_TILE = 256


def _split_attn(proj, b, t):
    o = 0
    out = []
    for width, shape in ((Q_COLS, (N_HEADS, HEAD_DIM)), (KV_COLS, (N_KV_HEADS, HEAD_DIM)),
                         (KV_COLS, (N_KV_HEADS, HEAD_DIM)), (IQ_COLS, (N_IDX_HEADS, IDX_DIM)),
                         (IDX_DIM, (IDX_DIM,)), (N_IDX_HEADS, (N_IDX_HEADS,))):
        out.append(proj[:, o:o + width].reshape((b, t) + shape))
        o += width
    return out


def kernel(x_prompt, x_sample, cache_k, cache_v, cache_idx_k, state_ssm_re, state_ssm_im, state_ffn_conv,
           page_table, w_attn_in, w_attn_out, ssm_a_re, ssm_a_im, ssm_log_dt, ssm_b_re, ssm_b_im, ssm_c_re,
           ssm_c_im, ssm_d, w_glu, b_glu, norm_mixer, norm_ffn, w_ffn_up, ffn_conv_w, ffn_conv_b, w_ffn_down,
           norm_final):
    bp, sp, d = x_prompt.shape
    bs, ss, _ = x_sample.shape
    tp, ts = bp * sp, bs * ss
    xp = x_prompt.reshape(tp, d)
    xs = x_sample.reshape(ts, d)

    n_in = w_attn_in.shape[1]
    n_in_pad = -(-n_in // LANE) * LANE
    w_in = jnp.pad(w_attn_in, ((0, 0), (0, n_in_pad - n_in))).astype(BF16)
    w_out = w_attn_out.astype(BF16)
    q_p, k_p, v_p, iq_p, ik_p, iw_p = _split_attn(norm_matmul(xp, norm_mixer[0], w_in, PROMPT_TILE), bp, sp)
    q_s, k_s, v_s, iq_s, ik_s, iw_s = _split_attn(norm_matmul(xs, norm_mixer[0], w_in, ts), bs, ss)
    o_p = dsa_prompt_attention(q_p, k_p, v_p, iq_p, ik_p, iw_p)
    o_s = dsa_sample_attention(q_s, k_s, v_s, iq_s, ik_s, iw_s, cache_k, cache_v, cache_idx_k, page_table)
    xp = matmul_residual(o_p.reshape(tp, Q_COLS), w_out, xp, PROMPT_TILE)
    xs = matmul_residual(o_s.reshape(ts, Q_COLS), w_out, xs, ts)

    tiles_per_seq = sp // PROMPT_TILE
    xp, gt_p0 = conv_ffn(xp, norm_ffn[0], w_ffn_up[0], ffn_conv_w[0], ffn_conv_b[0], w_ffn_down[0],
                         tm=PROMPT_TILE, seq_len=sp)
    xs, gt_s0 = conv_ffn(xs, norm_ffn[0], w_ffn_up[0], ffn_conv_w[0], ffn_conv_b[0], w_ffn_down[0],
                         tm=ts, seq_len=ss, halo=state_ffn_conv[0])

    disc = _s5_discretize(ssm_a_re, ssm_a_im, ssm_log_dt, ssm_b_re, ssm_b_im)
    xp3, sre_p, sim_p = s5_layer(xp.reshape(bp, sp, d), norm_mixer[1], disc, ssm_c_re, ssm_c_im, ssm_d,
                                 w_glu, b_glu, tm=S5_TILE)
    xs3, sre_s, sim_s = s5_layer(xs.reshape(bs, ss, d), norm_mixer[1], disc, ssm_c_re, ssm_c_im, ssm_d,
                                 w_glu, b_glu, s0=(state_ssm_re, state_ssm_im))

    yp, gt_p1 = conv_ffn(xp3.reshape(tp, d), norm_ffn[1], w_ffn_up[1], ffn_conv_w[1], ffn_conv_b[1],
                         w_ffn_down[1], tm=PROMPT_TILE, seq_len=sp, g_final=norm_final)
    ys, gt_s1 = conv_ffn(xs3.reshape(ts, d), norm_ffn[1], w_ffn_up[1], ffn_conv_w[1], ffn_conv_b[1],
                         w_ffn_down[1], tm=ts, seq_len=ss, halo=state_ffn_conv[1], g_final=norm_final)

    f = w_ffn_down.shape[1]
    keep = CONV_W - 1

    def conv_state_p(gt):
        return gt.reshape(bp, tiles_per_seq, SUBLANE, f)[:, -1, SUBLANE - keep:]

    def conv_state_s(gt):
        return gt.reshape(bs, ss, f)[:, ss - keep:]

    conv_prompt = jnp.stack([conv_state_p(gt_p0), conv_state_p(gt_p1)])
    conv_sample = jnp.stack([conv_state_s(gt_s0), conv_state_s(gt_s1)])
    return (yp.reshape(bp, sp, d), ys.reshape(bs, ss, d), k_p, v_p, ik_p, k_s, v_s, ik_s,
            sre_p, sim_p, sre_s, sim_s, conv_prompt, conv_sample)
```

```python
import functools
import math

import jax
import jax.numpy as jnp
from jax import lax
from jax.experimental import pallas as pl
from jax.experimental.pallas import tpu as pltpu

F32 = jnp.float32
BF16 = jnp.bfloat16

N_HEADS = 8
HEAD_DIM = 128
N_KV_HEADS = 2
N_IDX_HEADS = 4
IDX_DIM = 64
TOPK_MAX = 256
Q_BLOCK = 128
PAGE_SIZE = 128
SSM_GROUP = 16
SSM_STATE = 64
CONV_W = 3
NORM_EPS = 1e-6
NEG_INF = -1e30

LANE = 128
SUBLANE = 8
VMEM_LIMIT_BYTES = 56 * 1024 * 1024


def _params(*sem):
    return pltpu.CompilerParams(dimension_semantics=sem, vmem_limit_bytes=VMEM_LIMIT_BYTES)


def _rmsnorm(x, g):
    return x * lax.rsqrt(jnp.mean(x * x, axis=-1, keepdims=True) + NORM_EPS) * g


def _sigmoid(x):
    return 0.5 * jnp.tanh(0.5 * x) + 0.5


def _const_spec(shape):
    nd = len(shape)
    return pl.BlockSpec(shape, lambda *_: (0,) * nd)


def _norm_matmul_kernel(x_ref, g_ref, w_ref, o_ref):
    h = _rmsnorm(x_ref[...], g_ref[...])
    o_ref[...] = jnp.dot(h.astype(BF16), w_ref[...], preferred_element_type=F32)


def norm_matmul(x, g, w, tm):
    t, d = x.shape
    n = w.shape[1]
    return pl.pallas_call(
        _norm_matmul_kernel,
        grid=(t // tm,),
        in_specs=[pl.BlockSpec((tm, d), lambda i: (i, 0)), _const_spec((1, d)), _const_spec((d, n))],
        out_specs=pl.BlockSpec((tm, n), lambda i: (i, 0)),
        out_shape=jax.ShapeDtypeStruct((t, n), F32),
        compiler_params=_params("parallel"),
        name="norm_matmul",
    )(x, g.reshape(1, d), w)


def _matmul_residual_kernel(a_ref, w_ref, x_ref, o_ref):
    o_ref[...] = x_ref[...] + jnp.dot(a_ref[...], w_ref[...], preferred_element_type=F32)


def matmul_residual(a, w, x, tm):
    t, k = a.shape
    n = w.shape[1]
    return pl.pallas_call(
        _matmul_residual_kernel,
        grid=(t // tm,),
        in_specs=[pl.BlockSpec((tm, k), lambda i: (i, 0)), _const_spec((k, n)),
                  pl.BlockSpec((tm, n), lambda i: (i, 0))],
        out_specs=pl.BlockSpec((tm, n), lambda i: (i, 0)),
        out_shape=jax.ShapeDtypeStruct((t, n), F32),
        compiler_params=_params("parallel"),
        name="matmul_residual",
    )(a, w, x)


FFN_ACT_SLABS = 4


def _ffn_kernel(*refs, tm, nc, seq_tiles, seq_len, halo, final_norm):
    refs = list(refs)
    x_ref, gn_ref, wg_ref, wu_ref, cw_ref, cb_ref, wd_ref = refs[:7]
    pos = 7
    if halo:
        h1_ref, h2_ref = refs[pos:pos + 2]
        pos += 2
    if final_norm:
        gf_ref = refs[pos]
        pos += 1
    y_ref, gt_ref = refs[pos:pos + 2]
    h_scr, acc, g_a, u_a, g_b, u_b, act_buf = refs[pos + 2:pos + 9]
    if not halo:
        carry = refs[pos + 9]

    x = x_ref[...]
    h_scr[...] = _rmsnorm(x, gn_ref[...]).astype(BF16)
    acc[...] = jnp.zeros_like(acc)
    if halo:
        t_in_seq = lax.broadcasted_iota(jnp.int32, (tm, 1), 0) % seq_len
    else:
        first = (pl.program_id(0) % seq_tiles) == 0

    def up_g(c, gbuf):
        gbuf[pl.ds(SUBLANE, tm), :] = jnp.dot(h_scr[...], wg_ref[c], preferred_element_type=F32)

    def up_u(c, ubuf):
        ubuf[...] = jnp.dot(h_scr[...], wu_ref[c], preferred_element_type=F32)

    n_slabs = FFN_ACT_SLABS if tm % (FFN_ACT_SLABS * SUBLANE) == 0 else 1
    rs = tm // n_slabs

    def halo_rows(c, gbuf):
        if halo:
            gbuf[pl.ds(0, SUBLANE), :] = jnp.zeros((SUBLANE, gbuf.shape[1]), F32)
        else:
            gbuf[pl.ds(0, SUBLANE), :] = jnp.where(first, 0.0, carry[c])
            tail = gbuf[pl.ds(tm, SUBLANE), :]
            carry[c] = tail
            gt_ref[0, c] = tail

    def act_rows(c, gbuf, ubuf, r):
        r0 = r * rs
        g = gbuf[pl.ds(SUBLANE + r0, rs), :]
        gm1 = gbuf[pl.ds(SUBLANE - 1 + r0, rs), :]
        gm2 = gbuf[pl.ds(SUBLANE - 2 + r0, rs), :]
        if halo:
            t = t_in_seq[r0:r0 + rs]
            gm1 = jnp.where(t >= 1, gm1, h1_ref[c, r0:r0 + rs, :])
            gm2 = jnp.where(t >= 2, gm2, h2_ref[c, r0:r0 + rs, :])
            gt_ref[0, c, r0:r0 + rs, :] = g
        cw = cw_ref[c]
        gc = cw[0:1, :] * gm2 + cw[1:2, :] * gm1 + cw[2:3, :] * g + cb_ref[c]
        act_buf[r0:r0 + rs, :] = (gc * _sigmoid(gc) * ubuf[r0:r0 + rs, :]).astype(BF16)

    def down(c):
        acc[...] += jnp.dot(act_buf[...], wd_ref[c], preferred_element_type=F32)

    def half(c_act, gbuf, ubuf, c_up, gnext, unext):
        halo_rows(c_act, gbuf)
        if c_up is not None:
            up_g(c_up, gnext)
        for r in range(n_slabs):
            if c_up is not None and r == n_slabs // 2:
                up_u(c_up, unext)
            act_rows(c_act, gbuf, ubuf, r)
        down(c_act)

    up_g(0, g_a)
    up_u(0, u_a)
    bufs = ((g_a, u_a), (g_b, u_b))
    for c in range(nc):
        cur, nxt = bufs[c % 2], bufs[(c + 1) % 2]
        if c + 1 < nc:
            half(c, cur[0], cur[1], c + 1, nxt[0], nxt[1])
        else:
            half(c, cur[0], cur[1], None, None, None)
    y = x + acc[...]
    if final_norm:
        y = _rmsnorm(y, gf_ref[...])
    y_ref[...] = y


def conv_ffn(x, gn, w_up, conv_w, conv_b, w_down, *, tm, seq_len, halo=None, g_final=None, fc=256):
    t, d = x.shape
    f = w_down.shape[0]
    nc = f // fc
    wg = w_up[:, :f].astype(BF16).reshape(d, nc, fc).transpose(1, 0, 2)
    wu = w_up[:, f:].astype(BF16).reshape(d, nc, fc).transpose(1, 0, 2)
    wd = w_down.astype(BF16).reshape(nc, fc, d)
    cw = jnp.pad(conv_w, ((0, SUBLANE - CONV_W), (0, 0))).reshape(SUBLANE, nc, fc).transpose(1, 0, 2)
    cb = conv_b.reshape(nc, 1, fc)
    n_tiles = t // tm
    args = [x, gn.reshape(1, d), wg, wu, cw, cb, wd]
    in_specs = [pl.BlockSpec((tm, d), lambda i: (i, 0)), _const_spec((1, d)),
                _const_spec((nc, d, fc)), _const_spec((nc, d, fc)), _const_spec((nc, SUBLANE, fc)),
                _const_spec((nc, 1, fc)), _const_spec((nc, fc, d))]
    scratch = [pltpu.VMEM((tm, d), BF16), pltpu.VMEM((tm, d), F32),
               pltpu.VMEM((tm + SUBLANE, fc), F32), pltpu.VMEM((tm, fc), F32),
               pltpu.VMEM((tm + SUBLANE, fc), F32), pltpu.VMEM((tm, fc), F32),
               pltpu.VMEM((tm, fc), BF16)]
    if halo is not None:
        assert n_tiles == 1 and seq_len >= CONV_W - 1
        n_seq = t // seq_len
        z = jnp.zeros((n_seq, seq_len, f), F32)
        h1 = z.at[:, 0].set(halo[:, 1]).reshape(t, nc, fc).transpose(1, 0, 2)
        h2 = z.at[:, 0].set(halo[:, 0]).at[:, 1].set(halo[:, 1]).reshape(t, nc, fc).transpose(1, 0, 2)
        args += [h1, h2]
        in_specs += [_const_spec((nc, tm, fc)), _const_spec((nc, tm, fc))]
        tail_rows = tm
        seq_tiles = 1
    else:
        assert seq_len % tm == 0
        tail_rows = SUBLANE
        seq_tiles = seq_len // tm
        scratch.append(pltpu.VMEM((nc, SUBLANE, fc), F32))
    if g_final is not None:
        args.append(g_final.reshape(1, d))
        in_specs.append(_const_spec((1, d)))
    kern = functools.partial(_ffn_kernel, tm=tm, nc=nc, seq_tiles=seq_tiles, seq_len=seq_len,
                             halo=halo is not None, final_norm=g_final is not None)
    y, gt = pl.pallas_call(
        kern,
        grid=(n_tiles,),
        in_specs=in_specs,
        out_specs=[pl.BlockSpec((tm, d), lambda i: (i, 0)),
                   pl.BlockSpec((1, nc, tail_rows, fc), lambda i: (i, 0, 0, 0))],
        out_shape=[jax.ShapeDtypeStruct((t, d), F32),
                   jax.ShapeDtypeStruct((n_tiles, nc, tail_rows, fc), F32)],
        scratch_shapes=scratch,
        compiler_params=_params("arbitrary"),
        name="conv_ffn",
    )(*args)
    gt = gt.transpose(0, 2, 1, 3).reshape(n_tiles, tail_rows, f)
    return y, gt


S5_LANE_CHUNK = 512
S5_IN_BLOCK = 256
S5_OUT_GROUPS = 8


def _s5_kernel(*refs, tm, nseq, nsteps, chained):
    refs = list(refs)
    (x_ref, gn_ref, bb_ref, lam_ref, pw_ref, cre_ref, cim_ref, d_ref, wglu_ref, bglu_ref) = refs[:10]
    pos = 10
    if not chained:
        s0_ref = refs[pos]
        pos += 1
    y_ref, sfin_ref = refs[pos:pos + 2]
    xp, hp, sre, sim, yacc = refs[pos + 2:pos + 7]
    if chained:
        cin, st_carry = refs[pos + 7:pos + 9]
    n_state = sre.shape[1]
    d_model = x_ref.shape[-1]

    xp[...] = x_ref[0]
    h = _rmsnorm(xp[...], gn_ref[...])
    hp[...] = h.astype(BF16)

    n_in_blocks = d_model // S5_IN_BLOCK
    wcols = n_state // n_in_blocks
    for blk in range(n_in_blocks):
        bu = jnp.dot(hp[:, blk * S5_IN_BLOCK:(blk + 1) * S5_IN_BLOCK], bb_ref[blk],
                     preferred_element_type=F32)
        sre[:, blk * wcols:(blk + 1) * wcols] = bu[:, :wcols]
        sim[:, blk * wcols:(blk + 1) * wcols] = bu[:, wcols:]

    if chained:
        @pl.when(pl.program_id(1) == 0)
        def _():
            st_carry[...] = jnp.zeros_like(st_carry)

    lc = S5_LANE_CHUNK
    for c0 in range(0, n_state, lc):
        cols = slice(c0, c0 + lc)
        lr = jnp.broadcast_to(lam_ref[0:1, cols], (nseq, lc))
        li = jnp.broadcast_to(lam_ref[1:2, cols], (nseq, lc))
        if chained:
            init = (jnp.zeros((nseq, lc), F32), jnp.zeros((nseq, lc), F32))
        else:
            init = (s0_ref[0, :, cols], s0_ref[1, :, cols])

        cr, ci = init
        for i in range(nsteps):
            rows = slice(i * nseq, (i + 1) * nseq)
            cr, ci = lr * cr - li * ci + sre[rows, cols], lr * ci + li * cr + sim[rows, cols]
            sre[rows, cols] = cr
            sim[rows, cols] = ci

        if not chained:
            sfin_ref[0, :, cols] = cr
            sfin_ref[1, :, cols] = ci
        else:
            last = (nsteps - 1) * nseq
            pr_l = pw_ref[0, nsteps - 1:nsteps, cols]
            pi_l = pw_ref[1, nsteps - 1:nsteps, cols]
            c_r = st_carry[0:1, cols]
            c_i = st_carry[1:2, cols]
            for seq in range(nseq):
                cin[0, seq:seq + 1, :] = c_r
                cin[1, seq:seq + 1, :] = c_i
                e_r = sre[last + seq:last + seq + 1, cols]
                e_i = sim[last + seq:last + seq + 1, cols]
                c_r, c_i = pr_l * c_r - pi_l * c_i + e_r, pr_l * c_i + pi_l * c_r + e_i
            st_carry[0:1, cols] = c_r
            st_carry[1:2, cols] = c_i
            in_r = cin[0]
            in_i = cin[1]

            for i in range(nsteps):
                rows = slice(i * nseq, (i + 1) * nseq)
                pr = jnp.broadcast_to(pw_ref[0, i:i + 1, cols], (nseq, lc))
                pi = jnp.broadcast_to(pw_ref[1, i:i + 1, cols], (nseq, lc))
                sre[rows, cols] = sre[rows, cols] + (pr * in_r - pi * in_i)
                sim[rows, cols] = sim[rows, cols] + (pr * in_i + pi * in_r)

    if chained:
        sfin_ref[0] = st_carry[...]

    kcols = S5_OUT_GROUPS * SSM_STATE
    ncols = S5_OUT_GROUPS * SSM_GROUP
    for k in range(n_state // kcols):
        yk = jnp.dot(sre[:, k * kcols:(k + 1) * kcols].astype(BF16), cre_ref[k], preferred_element_type=F32)
        yk += jnp.dot(sim[:, k * kcols:(k + 1) * kcols].astype(BF16), cim_ref[k], preferred_element_type=F32)
        yacc[:, k * ncols:(k + 1) * ncols] = yk
    y = yacc[...] + d_ref[...] * _rmsnorm(xp[...], gn_ref[...])
    z = jnp.dot(jax.nn.gelu(y).astype(BF16), wglu_ref[...], preferred_element_type=F32) + bglu_ref[...]
    out = xp[...] + z[:, :d_model] * jax.nn.sigmoid(z[:, d_model:])
    y_ref[0] = out


def _s5_discretize(a_re, a_im, log_dt, b_re, b_im):
    a_re, a_im = a_re.astype(F32), a_im.astype(F32)
    dt = jnp.exp(log_dt.astype(F32))[:, None]
    mag = jnp.exp(a_re * dt)
    lam_re, lam_im = mag * jnp.cos(a_im * dt), mag * jnp.sin(a_im * dt)
    den = a_re * a_re + a_im * a_im
    n_re, n_im = lam_re - 1.0, lam_im
    f_re = (n_re * a_re + n_im * a_im) / den
    f_im = (n_im * a_re - n_re * a_im) / den
    b_re, b_im = b_re.astype(F32), b_im.astype(F32)
    bb_re = f_re[..., None] * b_re - f_im[..., None] * b_im
    bb_im = f_re[..., None] * b_im + f_im[..., None] * b_re
    return lam_re, lam_im, bb_re, bb_im


def _s5_weights(lam_re, lam_im, bb_re, bb_im, c_re, c_im, nsteps):
    n_groups = lam_re.shape[0]
    n_state = n_groups * SSM_STATE
    pr, pi = [lam_re.reshape(-1)], [lam_im.reshape(-1)]
    for _ in range(nsteps - 1):
        pr, pi = pr + [pr[-1] * pr[0] - pi[-1] * pi[0]], pi + [pr[-1] * pi[0] + pi[-1] * pr[0]]
    pw = jnp.stack([jnp.stack(pr), jnp.stack(pi)])
    lam = jnp.stack([lam_re.reshape(-1), lam_im.reshape(-1)])
    gpb = S5_IN_BLOCK // SSM_GROUP
    nb = n_groups // gpb
    eye = jnp.eye(gpb, dtype=F32)

    def bdiag_in(bb):
        bb = bb.reshape(nb, gpb, SSM_STATE, SSM_GROUP)
        return jnp.einsum('ngpc,gh->ngchp', bb, eye).reshape(nb, gpb * SSM_GROUP, gpb * SSM_STATE)

    bmat = jnp.concatenate([bdiag_in(bb_re), bdiag_in(bb_im)], axis=-1).astype(BF16)
    go = S5_OUT_GROUPS
    eye_o = jnp.eye(go, dtype=F32)

    def bdiag_out(cc):
        cc = cc.astype(F32).reshape(n_groups // go, go, SSM_GROUP, SSM_STATE)
        return jnp.einsum('ngcp,gh->ngphc', cc, eye_o).reshape(n_groups // go, go * SSM_STATE, go * SSM_GROUP)

    return lam, pw, bmat, bdiag_out(c_re).astype(BF16), bdiag_out(-c_im.astype(F32)).astype(BF16)


def s5_layer(x, gn, disc, c_re, c_im, d_skip, w_glu, b_glu, *, tm=None, s0=None):
    b, s, d = x.shape
    lam_re, lam_im, bb_re, bb_im = disc
    n_groups = lam_re.shape[0]
    n_state = n_groups * SSM_STATE
    chained = s0 is None
    if chained:
        nseq, nsteps = SUBLANE, tm // SUBLANE
        grid = (b, s // tm)
        xin = x.reshape(b, s // tm, nseq, nsteps, d).transpose(0, 1, 3, 2, 4).reshape(b, s, d)
        x_spec = pl.BlockSpec((1, tm, d), lambda i, j: (i, j, 0))
        sfin_spec = pl.BlockSpec((1, 2, n_state), lambda i, j: (i, 0, 0))
        sfin_shape = jax.ShapeDtypeStruct((b, 2, n_state), F32)
    else:
        nseq, nsteps, tm = b, s, b * s
        grid = (1, 1)
        xin = x.transpose(1, 0, 2).reshape(1, tm, d)
        x_spec = pl.BlockSpec((1, tm, d), lambda i, j: (0, 0, 0))
        sfin_spec = _const_spec((2, nseq, n_state))
        sfin_shape = jax.ShapeDtypeStruct((2, nseq, n_state), F32)
    lam, pw, bmat, cre, cim = _s5_weights(lam_re, lam_im, bb_re, bb_im, c_re, c_im, nsteps)
    args = [xin, gn.reshape(1, d), bmat, lam, pw, cre, cim, d_skip.reshape(1, d).astype(F32),
            w_glu.astype(BF16), b_glu.reshape(1, 2 * d).astype(F32)]
    in_specs = [x_spec, _const_spec((1, d)), _const_spec(bmat.shape), _const_spec(lam.shape),
                _const_spec(pw.shape), _const_spec(cre.shape), _const_spec(cim.shape),
                _const_spec((1, d)), _const_spec((d, 2 * d)), _const_spec((1, 2 * d))]
    scratch = [pltpu.VMEM((tm, d), F32), pltpu.VMEM((tm, d), BF16), pltpu.VMEM((tm, n_state), F32),
               pltpu.VMEM((tm, n_state), F32), pltpu.VMEM((tm, d), F32)]
    if chained:
        scratch += [pltpu.VMEM((2, nseq, S5_LANE_CHUNK), F32), pltpu.VMEM((2, n_state), F32)]
    else:
        args.append(jnp.stack([s0[0].reshape(b, n_state), s0[1].reshape(b, n_state)]).astype(F32))
        in_specs.append(_const_spec((2, nseq, n_state)))
    kern = functools.partial(_s5_kernel, tm=tm, nseq=nseq, nsteps=nsteps, chained=chained)
    y, sfin = pl.pallas_call(
        kern,
        grid=grid,
        in_specs=in_specs,
        out_specs=[x_spec, sfin_spec],
        out_shape=[jax.ShapeDtypeStruct(xin.shape, F32), sfin_shape],
        scratch_shapes=scratch,
        compiler_params=_params("arbitrary", "arbitrary"),
        name="s5_layer",
    )(*args)
    if chained:
        y = y.reshape(b, s // tm, nsteps, nseq, d).transpose(0, 1, 3, 2, 4).reshape(b, s, d)
        return (y, sfin[:, 0].reshape(b, n_groups, SSM_STATE), sfin[:, 1].reshape(b, n_groups, SSM_STATE))
    y = y.reshape(s, b, d).transpose(1, 0, 2)
    return y, sfin[0].reshape(b, n_groups, SSM_STATE), sfin[1].reshape(b, n_groups, SSM_STATE)


INT_MIN = -2 ** 31
N_KEY_BITS = 32


def _candidate(prefix, i):
    cand = prefix | lax.shift_left(jnp.int32(1), 31 - i)
    u = cand ^ jnp.int32(INT_MIN)
    bits = jnp.where(u >= 0, u, u ^ jnp.int32(0x7FFFFFFF))
    return cand, lax.bitcast_convert_type(bits, F32)


def _key_to_float(prefix):
    u = prefix ^ jnp.int32(INT_MIN)
    return lax.bitcast_convert_type(jnp.where(u >= 0, u, u ^ jnp.int32(0x7FFFFFFF)), F32)


KEY_CHUNK = 512
ATTN_KEY_STEP = KEY_CHUNK // 2
ATTN_SLAB = 64
QK_ROWS = 256
COUNT_ROWS = 128
COUNT_ACCS = 4
HEADS_PER_DOT = 2
LOG2_E = 1.4426950408889634


def _dsa_prompt_kernel(q_ref, k_ref, vt_ref, iq_ref, ik_ref, iw_ref, o_ref, sc, thr_scr, m_scr, l_scr, acc,
                       st_a, st_b, p_buf, *, topk, nkc):
    qb = Q_BLOCK
    kc = KEY_CHUNK
    hpd = HEADS_PER_DOT
    pairs_per_kv = N_HEADS // N_KV_HEADS // hpd
    j = (nkc - 1) * (kc // qb) + pl.program_id(1)
    nkb = j + 1

    def key_rows(kb):
        return pl.ds(pl.multiple_of(kb * qb, qb), qb)

    def chunk_rows(c):
        return slice(c * kc, (c + 1) * kc)

    kpos = lax.broadcasted_iota(jnp.int32, (kc, qb), 0)
    qpos = j * qb + lax.broadcasted_iota(jnp.int32, (kc, qb), 1)

    @pl.when(nkb * qb <= topk)
    def _():
        sc[chunk_rows(0), :] = jnp.where(kpos <= qpos, 0.0, NEG_INF)
        thr_scr[...] = jnp.zeros_like(thr_scr)

    @pl.when(nkb * qb > topk)
    def _():
        iq = iq_ref[0].reshape(N_IDX_HEADS * qb, IDX_DIM)
        iw = iw_ref[0]

        for c in range(nkc):
            dots = lax.dot_general(ik_ref[0, chunk_rows(c), :], iq, (((1,), (1,)), ((), ())),
                                   preferred_element_type=F32)
            s = jnp.zeros((kc, qb), F32)
            for h in range(N_IDX_HEADS):
                s = s + jnp.maximum(dots[:, h * qb:(h + 1) * qb], 0.0) * iw[h:h + 1, :]
            sc[chunk_rows(c), :] = jnp.where(kpos + c * kc <= qpos, s, NEG_INF)

        kf = jnp.float32(topk)

        def count(pred):
            cnt = jnp.zeros((COUNT_ACCS, SUBLANE, qb), F32)
            for c in range(nkc):
                for r0 in range(0, kc, COUNT_ROWS):
                    x = jnp.where(pred(sc[c * kc + r0:c * kc + r0 + COUNT_ROWS, :]), 1.0, 0.0)
                    x = x.reshape(COUNT_ROWS // SUBLANE // COUNT_ACCS, COUNT_ACCS, SUBLANE, qb)
                    cnt = cnt + jnp.sum(x, axis=0)
            return jnp.sum(jnp.sum(cnt, axis=0), axis=0, keepdims=True)

        def bit_body(i, prefix):
            cand, cf = _candidate(prefix, i)
            return jnp.where(count(lambda s: s >= cf) >= kf, cand, prefix)

        thr = _key_to_float(lax.fori_loop(0, N_KEY_BITS, bit_body, jnp.zeros((1, qb), jnp.int32)))
        thr_scr[...] = thr

        @pl.when(jnp.max(count(lambda s: s >= thr)) > kf)
        def _():
            need = kf - count(lambda s: s > thr)
            ki = lax.broadcasted_iota(jnp.int32, (qb, qb), 0)
            kj = lax.broadcasted_iota(jnp.int32, (qb, qb), 1)
            tri = jnp.where(kj <= ki, 1.0, 0.0).astype(BF16)

            def drop_blk(kb, run):
                s = sc[key_rows(kb), :]
                tie = s == thr
                pre = jnp.dot(tri, jnp.where(tie, 1.0, 0.0).astype(BF16), preferred_element_type=F32)
                sc[key_rows(kb), :] = jnp.where(tie & (pre + run > need), NEG_INF, s)
                return run + pre[qb - 1:qb, :]

            lax.fori_loop(0, nkb, drop_blk, jnp.zeros((1, qb), F32))

    m_scr[...] = jnp.full(m_scr.shape, NEG_INF, F32)
    l_scr[...] = jnp.zeros_like(l_scr)
    acc[...] = jnp.zeros_like(acc)
    thr = thr_scr[...]
    ks = ATTN_KEY_STEP
    slab = ATTN_SLAB
    cols = hpd * qb

    def step_rows(key0):
        return slice(key0, key0 + ks)

    def qk(key0, st_buf):
        for s0 in range(0, ks, QK_ROWS):
            rows = slice(key0 + s0, key0 + s0 + QK_ROWS)
            b1 = jnp.where(sc[rows, :] >= thr, 0.0, NEG_INF)
            bt = jnp.concatenate([b1] * hpd, axis=1)
            for g in range(N_KV_HEADS):
                kg = k_ref[0, g, rows, :]
                for pr in range(pairs_per_kv):
                    hp = g * pairs_per_kv + pr
                    qp = q_ref[0, hp * hpd:(hp + 1) * hpd].reshape(hpd * qb, HEAD_DIM)
                    st_buf[hp, s0:s0 + QK_ROWS, :] = lax.dot_general(
                        kg, qp, (((1,), (1,)), ((), ())), preferred_element_type=F32) + bt

    def slabs(ref, hp):
        for s0 in range(0, ks, slab):
            yield s0, ref[hp, s0:s0 + slab, :]

    def fold(x, op):
        x = x.reshape(slab // SUBLANE, SUBLANE, cols)
        return op(x, axis=0)

    def softmax_pv(key0, st_buf):
        for g in range(N_KV_HEADS):
            vtg = vt_ref[0, g, key0 // ks]
            for pr in range(pairs_per_kv):
                hp = g * pairs_per_kv + pr
                m_old = m_scr[hp]
                m8 = None
                for _, st in slabs(st_buf, hp):
                    f = fold(st, jnp.max)
                    m8 = f if m8 is None else jnp.maximum(m8, f)
                m_new = jnp.maximum(m_old, jnp.max(m8, axis=0, keepdims=True))
                alpha = jnp.exp2(m_old - m_new)
                l8 = jnp.zeros((SUBLANE, cols), F32)
                for s0, st in slabs(st_buf, hp):
                    p = jnp.exp2(st - m_new)
                    l8 = l8 + fold(p, jnp.sum)
                    p_buf[hp, s0:s0 + slab, :] = p.astype(BF16)
                l_scr[hp] = alpha * l_scr[hp] + jnp.sum(l8, axis=0, keepdims=True)
                acc[hp] = alpha * acc[hp] + jnp.dot(vtg, p_buf[hp], preferred_element_type=F32)
                m_scr[hp] = m_new

    qk(0, st_a)
    for c in range(nkc):
        key0 = c * kc
        qk(key0 + ks, st_b)
        softmax_pv(key0, st_a)
        if c + 1 < nkc:
            qk(key0 + 2 * ks, st_a)
        softmax_pv(key0 + ks, st_b)
    for hp in range(N_HEADS // hpd):
        ot = acc[hp] / l_scr[hp]
        for i in range(hpd):
            hd = hp * hpd + i
            o_ref[0, :, hd * HEAD_DIM:(hd + 1) * HEAD_DIM] = ot[:, i * qb:(i + 1) * qb].T.astype(o_ref.dtype)


def dsa_prompt_attention(q, k, v, iq, ik, iw):
    b, s = q.shape[:2]
    topk = min(TOPK_MAX, s // 4)
    assert topk % Q_BLOCK == 0 and topk <= KEY_CHUNK and s % KEY_CHUNK == 0
    q_hm = (q * (HEAD_DIM ** -0.5 * LOG2_E)).astype(BF16).transpose(0, 2, 1, 3)
    k_hm = k.astype(BF16).transpose(0, 2, 1, 3)
    n_steps = s // ATTN_KEY_STEP
    vt = v.astype(BF16).reshape(b, n_steps, ATTN_KEY_STEP, N_KV_HEADS, HEAD_DIM).transpose(0, 3, 1, 4, 2)
    iq_hm = iq.astype(BF16).transpose(0, 2, 1, 3)
    iw_t = (iw * (IDX_DIM ** -0.5 * N_IDX_HEADS ** -0.5)).transpose(0, 2, 1)
    return dsa_prompt_attention_from_layouts((q_hm, k_hm, vt, iq_hm, ik.astype(BF16), iw_t), b, s)


def dsa_prompt_attention_from_layouts(args, b, s):
    topk = min(TOPK_MAX, s // 4)
    assert topk % Q_BLOCK == 0 and topk <= KEY_CHUNK and s % KEY_CHUNK == 0
    n_dots = N_HEADS // HEADS_PER_DOT
    dot_cols = HEADS_PER_DOT * Q_BLOCK
    bpc = KEY_CHUNK // Q_BLOCK
    steps_per_chunk = KEY_CHUNK // ATTN_KEY_STEP
    return jnp.concatenate([_dsa_prompt_call(args, b, topk, nkc, bpc, steps_per_chunk, n_dots, dot_cols)
                            for nkc in range(1, s // KEY_CHUNK + 1)], axis=1)


def _dsa_prompt_call(args, b, topk, nkc, bpc, steps_per_chunk, n_dots, dot_cols):
    nk = nkc * KEY_CHUNK
    j0 = (nkc - 1) * bpc
    return pl.pallas_call(
        functools.partial(_dsa_prompt_kernel, topk=topk, nkc=nkc),
        grid=(b, bpc),
        in_specs=[
            pl.BlockSpec((1, N_HEADS, Q_BLOCK, HEAD_DIM), lambda i, j: (i, 0, j0 + j, 0)),
            pl.BlockSpec((1, N_KV_HEADS, nk, HEAD_DIM), lambda i, j: (i, 0, 0, 0)),
            pl.BlockSpec((1, N_KV_HEADS, nkc * steps_per_chunk, HEAD_DIM, ATTN_KEY_STEP),
                         lambda i, j: (i, 0, 0, 0, 0)),
            pl.BlockSpec((1, N_IDX_HEADS, Q_BLOCK, IDX_DIM), lambda i, j: (i, 0, j0 + j, 0)),
            pl.BlockSpec((1, nk, IDX_DIM), lambda i, j: (i, 0, 0)),
            pl.BlockSpec((1, args[5].shape[1], Q_BLOCK), lambda i, j: (i, 0, j0 + j)),
        ],
        out_specs=pl.BlockSpec((1, Q_BLOCK, N_HEADS * HEAD_DIM), lambda i, j: (i, j, 0)),
        out_shape=jax.ShapeDtypeStruct((b, bpc * Q_BLOCK, N_HEADS * HEAD_DIM), BF16),
        scratch_shapes=[pltpu.VMEM((nk, Q_BLOCK), F32), pltpu.VMEM((1, Q_BLOCK), F32),
                        pltpu.VMEM((n_dots, 1, dot_cols), F32),
                        pltpu.VMEM((n_dots, 1, dot_cols), F32),
                        pltpu.VMEM((n_dots, HEAD_DIM, dot_cols), F32),
                        pltpu.VMEM((n_dots, ATTN_KEY_STEP, dot_cols), F32),
                        pltpu.VMEM((n_dots, ATTN_KEY_STEP, dot_cols), F32),
                        pltpu.VMEM((n_dots, ATTN_KEY_STEP, dot_cols), BF16)],
        compiler_params=_params("arbitrary", "arbitrary"),
        name="dsa_prompt",
    )(*args)


PAGES_PER_CHUNK = 16
T_PAD = SUBLANE


RING_SLOTS = 4
SELECT_BATCH = 16


def _page_ring(pt_ref, streams, nch, buf, sem, dst_of_page, n_batches):
    cpp = PAGES_PER_CHUNK
    n = len(streams) * nch
    assert n % RING_SLOTS == 0
    b = pl.program_id(0)

    def copies(batch, i):
        src, c, slot = streams[i // nch], i % nch, i % RING_SLOTS
        return [pltpu.make_async_copy(src.at[pt_ref[batch, c * cpp + p]], dst_of_page(buf, slot, p),
                                      sem.at[slot]) for p in range(cpp)]

    def request(i):
        if i < n:
            for cp in copies(b, i):
                cp.start()
        else:
            @pl.when(b + 1 < n_batches)
            def _():
                for cp in copies(b + 1, i - n):
                    cp.start()

    def wait(i):
        for cp in copies(b, i):
            cp.wait()

    def prologue():
        @pl.when(b == 0)
        def _():
            for i in range(RING_SLOTS - 1):
                request(i)

    return request, wait, prologue


def _sample_scores_kernel(pt_ref, iq_ref, iw_ref, iknew_ref, cik_hbm, sc_ref, ikbuf, sem,
                          *, n_pages, n_new, n_batches):
    nch = n_pages // PAGES_PER_CHUNK
    cw = PAGES_PER_CHUNK * PAGE_SIZE
    request, wait, prologue = _page_ring(
        pt_ref, [cik_hbm], nch, ikbuf, sem,
        lambda buf, slot, p: buf.at[slot, :, pl.ds(p * PAGE_SIZE, PAGE_SIZE)], n_batches)
    iq = iq_ref[0]
    iw = iw_ref[0]

    def scores_of(ikc_t):
        dots = jnp.dot(iq, ikc_t, preferred_element_type=F32)
        s = jnp.zeros((T_PAD, ikc_t.shape[1]), F32)
        for h in range(N_IDX_HEADS):
            w = iw[h * T_PAD:(h + 1) * T_PAD, 0:1]
            s = s + jnp.maximum(dots[h * T_PAD:(h + 1) * T_PAD, :], 0.0) * w
        return s

    prologue()
    for i in range(nch):
        wait(i)
        request(i + RING_SLOTS - 1)
        sc_ref[0, i] = scores_of(ikbuf[i % RING_SLOTS].astype(BF16))
    kcol = lax.broadcasted_iota(jnp.int32, (T_PAD, PAGE_SIZE), 1)
    trow = lax.broadcasted_iota(jnp.int32, (T_PAD, PAGE_SIZE), 0)
    new_ok = (kcol <= trow) & (kcol < n_new)
    sc_ref[0, nch] = jnp.full((T_PAD, cw), NEG_INF, F32)
    sc_ref[0, nch, :, 0:PAGE_SIZE] = jnp.where(new_ok, scores_of(iknew_ref[0]), NEG_INF)


def _sample_select_kernel(sc_ref, sc2_ref, thr_ref, *, n_new, topk):
    sb, nblk, _, cw = sc_ref.shape
    kf = jnp.float32(topk)

    def count(pred):
        def blk(c, cnt):
            for t in range(cw // LANE):
                cnt = cnt + jnp.where(pred(sc_ref[:, c, :, t * LANE:(t + 1) * LANE]), 1.0, 0.0)
            return cnt

        cnt = lax.fori_loop(0, nblk, blk, jnp.zeros((sb, T_PAD, LANE), F32))
        return jnp.sum(cnt, axis=2, keepdims=True)

    def bit_body(i, prefix):
        cand, cf = _candidate(prefix, i)
        return jnp.where(count(lambda s: s >= cf) >= kf, cand, prefix)

    thr = _key_to_float(lax.fori_loop(0, N_KEY_BITS, bit_body, jnp.zeros((sb, T_PAD, 1), jnp.int32)))
    thr_ref[...] = jnp.broadcast_to(thr, thr_ref.shape)

    def copy_blk(c, _):
        sc2_ref[:, c] = sc_ref[:, c]
        return 0

    lax.fori_loop(0, nblk, copy_blk, 0)
    real = lax.broadcasted_iota(jnp.int32, (sb, T_PAD, 1), 1) < n_new

    @pl.when(jnp.max(jnp.where(real, count(lambda s: s >= thr), 0.0)) > kf)
    def _():
        need = kf - count(lambda s: s > thr)
        ki = lax.broadcasted_iota(jnp.int32, (LANE, LANE), 0)
        kj = lax.broadcasted_iota(jnp.int32, (LANE, LANE), 1)
        tri = jnp.where(ki <= kj, 1.0, 0.0).astype(BF16)

        def drop_blk(c, run):
            s = sc_ref[:, c]
            for t in range(cw // LANE):
                st = s[:, :, t * LANE:(t + 1) * LANE]
                tie = st == thr
                ind = jnp.where(tie, 1.0, 0.0).reshape(sb * T_PAD, LANE).astype(BF16)
                pre = jnp.dot(ind, tri, preferred_element_type=F32).reshape(sb, T_PAD, LANE)
                sc2_ref[:, c, :, t * LANE:(t + 1) * LANE] = jnp.where(tie & (pre + run > need), NEG_INF, st)
                run = run + pre[:, :, LANE - 1:LANE]
            return run

        lax.fori_loop(0, nblk, drop_blk, jnp.zeros((sb, T_PAD, 1), F32))


def _sample_attend_kernel(pt_ref, q_ref, knew_ref, vnew_ref, sc_ref, thr_ref, ck_hbm, cv_hbm, o_ref,
                          kvbuf, lg, sem, *, n_pages, n_batches):
    nch = n_pages // PAGES_PER_CHUNK
    cw = PAGES_PER_CHUNK * PAGE_SIZE
    rep = N_HEADS // N_KV_HEADS
    rows = rep * T_PAD
    page_rows = PAGE_SIZE * N_KV_HEADS
    request, wait, prologue = _page_ring(
        pt_ref, [ck_hbm, cv_hbm], nch, kvbuf, sem,
        lambda buf, slot, p: buf.at[slot, pl.ds(p * page_rows, page_rows), :], n_batches)
    thr = thr_ref[0][:, 0:1]

    def kv_rows(slot, g):
        return kvbuf[slot, pl.ds(g, cw, stride=N_KV_HEADS), :].astype(BF16)

    def logits_of(kg, g, s):
        qg = q_ref[0, g]
        lgt = lax.dot_general(qg, kg, (((1,), (1,)), ((), ())), preferred_element_type=F32)
        return lgt + jnp.concatenate([jnp.where(s >= thr, 0.0, NEG_INF)] * rep, axis=0)

    prologue()
    for i in range(nch):
        wait(i)
        request(i + RING_SLOTS - 1)
        for g in range(N_KV_HEADS):
            lg[g, i] = logits_of(kv_rows(i % RING_SLOTS, g), g, sc_ref[0, i])
    for g in range(N_KV_HEADS):
        lg[g, nch] = jnp.full((rows, cw), NEG_INF, F32)
        lg[g, nch, :, 0:PAGE_SIZE] = logits_of(knew_ref[0][:, g * HEAD_DIM:(g + 1) * HEAD_DIM], g,
                                               sc_ref[0, nch, :, 0:PAGE_SIZE])

    denom = []
    for g in range(N_KV_HEADS):
        mx = lax.fori_loop(0, nch + 1, lambda c, m, g=g: jnp.maximum(m, lg[g, c]),
                           jnp.full((rows, cw), NEG_INF, F32))
        mx = jnp.max(mx, axis=1, keepdims=True)

        def exp_blk(c, tot, g=g, mx=mx):
            p = jnp.exp(lg[g, c] - mx)
            lg[g, c] = p
            return tot + p

        tot = lax.fori_loop(0, nch + 1, exp_blk, jnp.zeros((rows, cw), F32))
        denom.append(jnp.sum(tot, axis=1, keepdims=True))
        o_ref[0, g] = jnp.dot(lg[g, nch, :, 0:PAGE_SIZE].astype(BF16),
                              vnew_ref[0][:, g * HEAD_DIM:(g + 1) * HEAD_DIM], preferred_element_type=F32)

    for i in range(nch, 2 * nch):
        wait(i)
        request(i + RING_SLOTS - 1)
        for g in range(N_KV_HEADS):
            o_ref[0, g] += jnp.dot(lg[g, i - nch].astype(BF16), kv_rows(i % RING_SLOTS, g),
                                   preferred_element_type=F32)
    for g in range(N_KV_HEADS):
        o_ref[0, g] = o_ref[0, g] / denom[g]


def dsa_sample_attention(q, k, v, iq, ik, iw, cache_k, cache_v, cache_idx_k, page_table):
    b, t = q.shape[:2]
    n_pages = page_table.shape[1]
    n_phys = cache_k.shape[0]
    n_keys = n_pages * PAGE_SIZE + t
    topk = min(TOPK_MAX, n_keys // 4)
    rep = N_HEADS // N_KV_HEADS
    assert t <= T_PAD and n_pages % PAGES_PER_CHUNK == 0 and n_pages * PAGE_SIZE >= topk
    nch = n_pages // PAGES_PER_CHUNK
    cw = PAGES_PER_CHUNK * PAGE_SIZE

    def pad_t(x):
        return jnp.pad(x, ((0, 0), (0, T_PAD - t)) + ((0, 0),) * (x.ndim - 2))

    def pad_page(x):
        return jnp.pad(x, ((0, 0), (0, PAGE_SIZE - t), (0, 0)))

    kvc = N_KV_HEADS * HEAD_DIM
    q_s = pad_t((q * HEAD_DIM ** -0.5).astype(BF16)).reshape(b, T_PAD, N_KV_HEADS, rep, HEAD_DIM)
    q_s = q_s.transpose(0, 2, 3, 1, 4).reshape(b, N_KV_HEADS, rep * T_PAD, HEAD_DIM)
    iq_s = pad_t(iq.astype(BF16)).transpose(0, 2, 1, 3).reshape(b, N_IDX_HEADS * T_PAD, IDX_DIM)
    iw_s = pad_t(iw * (IDX_DIM ** -0.5 * N_IDX_HEADS ** -0.5)).transpose(0, 2, 1)
    iw_s = jnp.broadcast_to(iw_s.reshape(b, N_IDX_HEADS * T_PAD, 1), (b, N_IDX_HEADS * T_PAD, LANE))
    k_new = pad_page(k.reshape(b, t, kvc).astype(BF16))
    v_new = pad_page(v.reshape(b, t, kvc).astype(BF16))
    ik_new = pad_page(ik.astype(BF16)).transpose(0, 2, 1)

    def bspec(shape):
        nd = len(shape)
        return pl.BlockSpec((1,) + shape, lambda i, pt: (i,) + (0,) * nd)

    any_spec = pl.BlockSpec(memory_space=pl.ANY)
    sc_shape = (nch + 1, T_PAD, cw)
    sc = pl.pallas_call(
        functools.partial(_sample_scores_kernel, n_pages=n_pages, n_new=t, n_batches=b),
        grid_spec=pltpu.PrefetchScalarGridSpec(
            num_scalar_prefetch=1,
            grid=(b,),
            in_specs=[bspec((N_IDX_HEADS * T_PAD, IDX_DIM)), bspec((N_IDX_HEADS * T_PAD, LANE)),
                      bspec((IDX_DIM, PAGE_SIZE)), any_spec],
            out_specs=bspec(sc_shape),
            scratch_shapes=[pltpu.VMEM((RING_SLOTS, IDX_DIM, cw), F32), pltpu.SemaphoreType.DMA((RING_SLOTS,))],
        ),
        out_shape=jax.ShapeDtypeStruct((b,) + sc_shape, F32),
        compiler_params=_params("arbitrary"),
        name="dsa_sample_scores",
    )(page_table, iq_s, iw_s, ik_new, cache_idx_k.transpose(0, 2, 1))

    sb = math.gcd(b, SELECT_BATCH)
    sc_blk = pl.BlockSpec((sb,) + sc_shape, lambda i: (i, 0, 0, 0))
    sc, thr = pl.pallas_call(
        functools.partial(_sample_select_kernel, n_new=t, topk=topk),
        grid=(b // sb,),
        in_specs=[sc_blk],
        out_specs=[sc_blk, pl.BlockSpec((sb, T_PAD, LANE), lambda i: (i, 0, 0))],
        out_shape=[jax.ShapeDtypeStruct((b,) + sc_shape, F32), jax.ShapeDtypeStruct((b, T_PAD, LANE), F32)],
        compiler_params=_params("arbitrary"),
        name="dsa_sample_select",
    )(sc)

    o = pl.pallas_call(
        functools.partial(_sample_attend_kernel, n_pages=n_pages, n_batches=b),
        grid_spec=pltpu.PrefetchScalarGridSpec(
            num_scalar_prefetch=1,
            grid=(b,),
            in_specs=[bspec((N_KV_HEADS, rep * T_PAD, HEAD_DIM)), bspec((PAGE_SIZE, kvc)), bspec((PAGE_SIZE, kvc)),
                      bspec(sc_shape), bspec((T_PAD, LANE)), any_spec, any_spec],
            out_specs=bspec((N_KV_HEADS, rep * T_PAD, HEAD_DIM)),
            scratch_shapes=[pltpu.VMEM((RING_SLOTS, cw * N_KV_HEADS, HEAD_DIM), F32),
                            pltpu.VMEM((N_KV_HEADS, nch + 1, rep * T_PAD, cw), F32),
                            pltpu.SemaphoreType.DMA((RING_SLOTS,))],
        ),
        out_shape=jax.ShapeDtypeStruct((b, N_KV_HEADS, rep * T_PAD, HEAD_DIM), F32),
        compiler_params=_params("arbitrary"),
        name="dsa_sample_attend",
    )(page_table, q_s, k_new, v_new, sc, thr,
      cache_k.reshape(n_phys, PAGE_SIZE * N_KV_HEADS, HEAD_DIM),
      cache_v.reshape(n_phys, PAGE_SIZE * N_KV_HEADS, HEAD_DIM))
    o = o.reshape(b, N_KV_HEADS, rep, T_PAD, HEAD_DIM)[:, :, :, :t]
    return o.transpose(0, 3, 1, 2, 4).reshape(b, t, N_HEADS * HEAD_DIM).astype(BF16)


Q_COLS = N_HEADS * HEAD_DIM
KV_COLS = N_KV_HEADS * HEAD_DIM
IQ_COLS = N_IDX_HEADS * IDX_DIM
IN_COLS = Q_COLS + 2 * KV_COLS + IQ_COLS + IDX_DIM + N_IDX_HEADS
PROMPT_TILE = 512
S5_TILE = 512


def _attn_in_proj_kernel(x_ref, g_ref, w_ref, q_ref, k_ref, v_ref, kh_ref, vt_ref, iq_ref, ik_ref, ikb_ref, iw_ref):
    tm = x_ref.shape[0]
    h = _rmsnorm(x_ref[...], g_ref[...])
    proj = jnp.dot(h.astype(BF16), w_ref[...], preferred_element_type=F32)
    for hd in range(N_HEADS):
        q_ref[0, hd] = (proj[:, hd * HEAD_DIM:(hd + 1) * HEAD_DIM] * (HEAD_DIM ** -0.5 * LOG2_E)).astype(BF16)
    k = proj[:, Q_COLS:Q_COLS + KV_COLS]
    v = proj[:, Q_COLS + KV_COLS:Q_COLS + 2 * KV_COLS]
    for g in range(N_KV_HEADS):
        k_ref[pl.ds(g, tm, stride=N_KV_HEADS), :] = k[:, g * HEAD_DIM:(g + 1) * HEAD_DIM]
        v_ref[pl.ds(g, tm, stride=N_KV_HEADS), :] = v[:, g * HEAD_DIM:(g + 1) * HEAD_DIM]
        kh_ref[0, g] = k[:, g * HEAD_DIM:(g + 1) * HEAD_DIM].astype(BF16)
        for r in range(tm // ATTN_KEY_STEP):
            blk = v[r * ATTN_KEY_STEP:(r + 1) * ATTN_KEY_STEP, g * HEAD_DIM:(g + 1) * HEAD_DIM]
            vt_ref[0, g, r] = blk.T.astype(BF16)
    o = Q_COLS + 2 * KV_COLS
    for hd in range(N_IDX_HEADS):
        iq_ref[0, hd] = proj[:, o + hd * IDX_DIM:o + (hd + 1) * IDX_DIM].astype(BF16)
    ik = proj[:, o + IQ_COLS:o + IQ_COLS + IDX_DIM]
    ik_ref[...] = ik
    ikb_ref[0] = ik.astype(BF16)
    iw = proj[:, o + IQ_COLS + IDX_DIM:o + IQ_COLS + IDX_DIM + LANE] * (IDX_DIM ** -0.5 * N_IDX_HEADS ** -0.5)
    iw_ref[0] = iw.T[0:SUBLANE, :]


def attn_in_proj_prompt(x, g, w_in, b, s, tm):
    t, d = x.shape
    n_pad = Q_COLS + 2 * KV_COLS + IQ_COLS + IDX_DIM + LANE
    w = jnp.pad(w_in, ((0, 0), (0, n_pad - w_in.shape[1]))).astype(BF16)
    tiles = s // tm
    n_steps = s // ATTN_KEY_STEP
    assert s % tm == 0 and tm % ATTN_KEY_STEP == 0

    def bi(i):
        return i // tiles, i % tiles

    outs = pl.pallas_call(
        _attn_in_proj_kernel,
        grid=(t // tm,),
        in_specs=[pl.BlockSpec((tm, d), lambda i: (i, 0)), _const_spec((1, d)), _const_spec((d, n_pad))],
        out_specs=[
            pl.BlockSpec((1, N_HEADS, tm, HEAD_DIM), lambda i: (bi(i)[0], 0, bi(i)[1], 0)),
            pl.BlockSpec((tm * N_KV_HEADS, HEAD_DIM), lambda i: (i, 0)),
            pl.BlockSpec((tm * N_KV_HEADS, HEAD_DIM), lambda i: (i, 0)),
            pl.BlockSpec((1, N_KV_HEADS, tm, HEAD_DIM), lambda i: (bi(i)[0], 0, bi(i)[1], 0)),
            pl.BlockSpec((1, N_KV_HEADS, tm // ATTN_KEY_STEP, HEAD_DIM, ATTN_KEY_STEP),
                         lambda i: (bi(i)[0], 0, bi(i)[1], 0, 0)),
            pl.BlockSpec((1, N_IDX_HEADS, tm, IDX_DIM), lambda i: (bi(i)[0], 0, bi(i)[1], 0)),
            pl.BlockSpec((tm, IDX_DIM), lambda i: (i, 0)),
            pl.BlockSpec((1, tm, IDX_DIM), lambda i: (bi(i)[0], bi(i)[1], 0)),
            pl.BlockSpec((1, SUBLANE, tm), lambda i: (bi(i)[0], 0, bi(i)[1])),
        ],
        out_shape=[
            jax.ShapeDtypeStruct((b, N_HEADS, s, HEAD_DIM), BF16),
            jax.ShapeDtypeStruct((t * N_KV_HEADS, HEAD_DIM), F32),
            jax.ShapeDtypeStruct((t * N_KV_HEADS, HEAD_DIM), F32),
            jax.ShapeDtypeStruct((b, N_KV_HEADS, s, HEAD_DIM), BF16),
            jax.ShapeDtypeStruct((b, N_KV_HEADS, n_steps, HEAD_DIM, ATTN_KEY_STEP), BF16),
            jax.ShapeDtypeStruct((b, N_IDX_HEADS, s, IDX_DIM), BF16),
            jax.ShapeDtypeStruct((t, IDX_DIM), F32),
            jax.ShapeDtypeStruct((b, s, IDX_DIM), BF16),
            jax.ShapeDtypeStruct((b, SUBLANE, s), F32),
        ],
        compiler_params=_params("parallel"),
        name="attn_in_proj",
    )(x, g.reshape(1, d), w)
    q_hm, k, v, k_hm, vt, iq_hm, ik, ik_b, iw_t = outs
    return k, v, ik, (q_hm, k_hm, vt, iq_hm, ik_b, iw_t)


def _split_attn(proj, b, t):
    o = 0
    out = []
    for width, shape in ((Q_COLS, (N_HEADS, HEAD_DIM)), (KV_COLS, (N_KV_HEADS, HEAD_DIM)),
                         (KV_COLS, (N_KV_HEADS, HEAD_DIM)), (IQ_COLS, (N_IDX_HEADS, IDX_DIM)),
                         (IDX_DIM, (IDX_DIM,)), (N_IDX_HEADS, (N_IDX_HEADS,))):
        out.append(proj[:, o:o + width].reshape((b, t) + shape))
        o += width
    return out


def kernel(x_prompt, x_sample, cache_k, cache_v, cache_idx_k, state_ssm_re, state_ssm_im, state_ffn_conv,
           page_table, w_attn_in, w_attn_out, ssm_a_re, ssm_a_im, ssm_log_dt, ssm_b_re, ssm_b_im, ssm_c_re,
           ssm_c_im, ssm_d, w_glu, b_glu, norm_mixer, norm_ffn, w_ffn_up, ffn_conv_w, ffn_conv_b, w_ffn_down,
           norm_final):
    bp, sp, d = x_prompt.shape
    bs, ss, _ = x_sample.shape
    tp, ts = bp * sp, bs * ss
    xp = x_prompt.reshape(tp, d)
    xs = x_sample.reshape(ts, d)

    n_in = w_attn_in.shape[1]
    n_in_pad = -(-n_in // LANE) * LANE
    w_in = jnp.pad(w_attn_in, ((0, 0), (0, n_in_pad - n_in))).astype(BF16)
    w_out = w_attn_out.astype(BF16)
    k_p, v_p, ik_p, prompt_layouts = attn_in_proj_prompt(xp, norm_mixer[0], w_attn_in, bp, sp, PROMPT_TILE)
    k_p = k_p.reshape(bp, sp, N_KV_HEADS, HEAD_DIM)
    v_p = v_p.reshape(bp, sp, N_KV_HEADS, HEAD_DIM)
    ik_p = ik_p.reshape(bp, sp, IDX_DIM)
    q_s, k_s, v_s, iq_s, ik_s, iw_s = _split_attn(norm_matmul(xs, norm_mixer[0], w_in, ts), bs, ss)
    o_p = dsa_prompt_attention_from_layouts(prompt_layouts, bp, sp)
    o_s = dsa_sample_attention(q_s, k_s, v_s, iq_s, ik_s, iw_s, cache_k, cache_v, cache_idx_k, page_table)
    xp = matmul_residual(o_p.reshape(tp, Q_COLS), w_out, xp, PROMPT_TILE)
    xs = matmul_residual(o_s.reshape(ts, Q_COLS), w_out, xs, ts)

    tiles_per_seq = sp // PROMPT_TILE
    xp, gt_p0 = conv_ffn(xp, norm_ffn[0], w_ffn_up[0], ffn_conv_w[0], ffn_conv_b[0], w_ffn_down[0],
                         tm=PROMPT_TILE, seq_len=sp)
    xs, gt_s0 = conv_ffn(xs, norm_ffn[0], w_ffn_up[0], ffn_conv_w[0], ffn_conv_b[0], w_ffn_down[0],
                         tm=ts, seq_len=ss, halo=state_ffn_conv[0])

    disc = _s5_discretize(ssm_a_re, ssm_a_im, ssm_log_dt, ssm_b_re, ssm_b_im)
    xp3, sre_p, sim_p = s5_layer(xp.reshape(bp, sp, d), norm_mixer[1], disc, ssm_c_re, ssm_c_im, ssm_d,
                                 w_glu, b_glu, tm=S5_TILE)
    xs3, sre_s, sim_s = s5_layer(xs.reshape(bs, ss, d), norm_mixer[1], disc, ssm_c_re, ssm_c_im, ssm_d,
                                 w_glu, b_glu, s0=(state_ssm_re, state_ssm_im))

    yp, gt_p1 = conv_ffn(xp3.reshape(tp, d), norm_ffn[1], w_ffn_up[1], ffn_conv_w[1], ffn_conv_b[1],
                         w_ffn_down[1], tm=PROMPT_TILE, seq_len=sp, g_final=norm_final)
    ys, gt_s1 = conv_ffn(xs3.reshape(ts, d), norm_ffn[1], w_ffn_up[1], ffn_conv_w[1], ffn_conv_b[1],
                         w_ffn_down[1], tm=ts, seq_len=ss, halo=state_ffn_conv[1], g_final=norm_final)

    f = w_ffn_down.shape[1]
    keep = CONV_W - 1

    def conv_state_p(gt):
        return gt.reshape(bp, tiles_per_seq, SUBLANE, f)[:, -1, SUBLANE - keep:]

    def conv_state_s(gt):
        return gt.reshape(bs, ss, f)[:, ss - keep:]

    conv_prompt = jnp.stack([conv_state_p(gt_p0), conv_state_p(gt_p1)])
    conv_sample = jnp.stack([conv_state_s(gt_s0), conv_state_s(gt_s1)])
    return (yp.reshape(bp, sp, d), ys.reshape(bs, ss, d), k_p, v_p, ik_p, k_s, v_s, ik_s,
            sre_p, sim_p, sre_s, sim_s, conv_prompt, conv_sample)
```

```python
import functools
import math

import jax
import jax.numpy as jnp
from jax import lax
from jax.experimental import pallas as pl
from jax.experimental.pallas import tpu as pltpu

F32 = jnp.float32
BF16 = jnp.bfloat16

N_HEADS = 8
HEAD_DIM = 128
N_KV_HEADS = 2
N_IDX_HEADS = 4
IDX_DIM = 64
TOPK_MAX = 256
Q_BLOCK = 128
PAGE_SIZE = 128
SSM_GROUP = 16
SSM_STATE = 64
CONV_W = 3
NORM_EPS = 1e-6
NEG_INF = -1e30

LANE = 128
SUBLANE = 8
VMEM_LIMIT_BYTES = 56 * 1024 * 1024


def _params(*sem):
    return pltpu.CompilerParams(dimension_semantics=sem, vmem_limit_bytes=VMEM_LIMIT_BYTES)


def _rmsnorm(x, g):
    return x * lax.rsqrt(jnp.mean(x * x, axis=-1, keepdims=True) + NORM_EPS) * g


def _sigmoid(x):
    return 0.5 * jnp.tanh(0.5 * x) + 0.5


def _const_spec(shape):
    nd = len(shape)
    return pl.BlockSpec(shape, lambda *_: (0,) * nd)


def _norm_matmul_kernel(x_ref, g_ref, w_ref, o_ref):
    h = _rmsnorm(x_ref[...], g_ref[...])
    o_ref[...] = jnp.dot(h.astype(BF16), w_ref[...], preferred_element_type=F32)


def norm_matmul(x, g, w, tm):
    t, d = x.shape
    n = w.shape[1]
    return pl.pallas_call(
        _norm_matmul_kernel,
        grid=(t // tm,),
        in_specs=[pl.BlockSpec((tm, d), lambda i: (i, 0)), _const_spec((1, d)), _const_spec((d, n))],
        out_specs=pl.BlockSpec((tm, n), lambda i: (i, 0)),
        out_shape=jax.ShapeDtypeStruct((t, n), F32),
        compiler_params=_params("parallel"),
        name="norm_matmul",
    )(x, g.reshape(1, d), w)


def _matmul_residual_kernel(a_ref, w_ref, x_ref, o_ref):
    o_ref[...] = x_ref[...] + jnp.dot(a_ref[...], w_ref[...], preferred_element_type=F32)


def matmul_residual(a, w, x, tm):
    t, k = a.shape
    n = w.shape[1]
    return pl.pallas_call(
        _matmul_residual_kernel,
        grid=(t // tm,),
        in_specs=[pl.BlockSpec((tm, k), lambda i: (i, 0)), _const_spec((k, n)),
                  pl.BlockSpec((tm, n), lambda i: (i, 0))],
        out_specs=pl.BlockSpec((tm, n), lambda i: (i, 0)),
        out_shape=jax.ShapeDtypeStruct((t, n), F32),
        compiler_params=_params("parallel"),
        name="matmul_residual",
    )(a, w, x)


FFN_ACT_SLABS = 4


def _ffn_kernel(*refs, tm, nc, seq_tiles, seq_len, halo, final_norm):
    refs = list(refs)
    x_ref, gn_ref, wup_ref, cw_ref, cb_ref, wd_ref = refs[:6]
    pos = 6
    fc = cw_ref.shape[2]
    f = nc * fc
    if halo:
        h1_ref, h2_ref = refs[pos:pos + 2]
        pos += 2
    if final_norm:
        gf_ref = refs[pos]
        pos += 1
    y_ref, gt_ref = refs[pos:pos + 2]
    h_scr, acc, g_a, u_a, g_b, u_b, act_buf = refs[pos + 2:pos + 9]
    if not halo:
        carry = refs[pos + 9]

    x = x_ref[...]
    h_scr[...] = _rmsnorm(x, gn_ref[...]).astype(BF16)
    acc[...] = jnp.zeros_like(acc)
    if halo:
        t_in_seq = lax.broadcasted_iota(jnp.int32, (tm, 1), 0) % seq_len
    else:
        first = (pl.program_id(0) % seq_tiles) == 0

    def up_g(c, gbuf):
        gbuf[pl.ds(SUBLANE, tm), :] = jnp.dot(h_scr[...], wup_ref[:, c * fc:(c + 1) * fc],
                                              preferred_element_type=F32)

    def up_u(c, ubuf):
        ubuf[...] = jnp.dot(h_scr[...], wup_ref[:, f + c * fc:f + (c + 1) * fc], preferred_element_type=F32)

    n_slabs = FFN_ACT_SLABS if tm % (FFN_ACT_SLABS * SUBLANE) == 0 else 1
    rs = tm // n_slabs

    def halo_rows(c, gbuf):
        if halo:
            gbuf[pl.ds(0, SUBLANE), :] = jnp.zeros((SUBLANE, gbuf.shape[1]), F32)
        else:
            gbuf[pl.ds(0, SUBLANE), :] = jnp.where(first, 0.0, carry[c])
            tail = gbuf[pl.ds(tm, SUBLANE), :]
            carry[c] = tail
            gt_ref[0, c] = tail

    def act_rows(c, gbuf, ubuf, r):
        r0 = r * rs
        g = gbuf[pl.ds(SUBLANE + r0, rs), :]
        gm1 = gbuf[pl.ds(SUBLANE - 1 + r0, rs), :]
        gm2 = gbuf[pl.ds(SUBLANE - 2 + r0, rs), :]
        if halo:
            t = t_in_seq[r0:r0 + rs]
            gm1 = jnp.where(t >= 1, gm1, h1_ref[c, r0:r0 + rs, :])
            gm2 = jnp.where(t >= 2, gm2, h2_ref[c, r0:r0 + rs, :])
            gt_ref[0, c, r0:r0 + rs, :] = g
        cw = cw_ref[c]
        gc = cw[0:1, :] * gm2 + cw[1:2, :] * gm1 + cw[2:3, :] * g + cb_ref[c]
        act_buf[r0:r0 + rs, :] = (gc * _sigmoid(gc) * ubuf[r0:r0 + rs, :]).astype(BF16)

    def down(c):
        acc[...] += jnp.dot(act_buf[...], wd_ref[c * fc:(c + 1) * fc, :], preferred_element_type=F32)

    def half(c_act, gbuf, ubuf, c_up, gnext, unext):
        halo_rows(c_act, gbuf)
        if c_up is not None:
            up_g(c_up, gnext)
        for r in range(n_slabs):
            if c_up is not None and r == n_slabs // 2:
                up_u(c_up, unext)
            act_rows(c_act, gbuf, ubuf, r)
        down(c_act)

    up_g(0, g_a)
    up_u(0, u_a)
    bufs = ((g_a, u_a), (g_b, u_b))
    for c in range(nc):
        cur, nxt = bufs[c % 2], bufs[(c + 1) % 2]
        if c + 1 < nc:
            half(c, cur[0], cur[1], c + 1, nxt[0], nxt[1])
        else:
            half(c, cur[0], cur[1], None, None, None)
    y = x + acc[...]
    if final_norm:
        y = _rmsnorm(y, gf_ref[...])
    y_ref[...] = y


def conv_ffn(x, gn, w_up, conv_w, conv_b, w_down, *, tm, seq_len, halo=None, g_final=None, fc=256):
    t, d = x.shape
    f = w_down.shape[0]
    nc = f // fc
    assert fc % LANE == 0 and f % fc == 0
    cw = jnp.pad(conv_w, ((0, SUBLANE - CONV_W), (0, 0))).reshape(SUBLANE, nc, fc).transpose(1, 0, 2)
    cb = conv_b.reshape(nc, 1, fc)
    n_tiles = t // tm
    args = [x, gn.reshape(1, d), w_up.astype(BF16), cw, cb, w_down.astype(BF16)]
    in_specs = [pl.BlockSpec((tm, d), lambda i: (i, 0)), _const_spec((1, d)),
                _const_spec((d, 2 * f)), _const_spec((nc, SUBLANE, fc)),
                _const_spec((nc, 1, fc)), _const_spec((f, d))]
    scratch = [pltpu.VMEM((tm, d), BF16), pltpu.VMEM((tm, d), F32),
               pltpu.VMEM((tm + SUBLANE, fc), F32), pltpu.VMEM((tm, fc), F32),
               pltpu.VMEM((tm + SUBLANE, fc), F32), pltpu.VMEM((tm, fc), F32),
               pltpu.VMEM((tm, fc), BF16)]
    if halo is not None:
        assert n_tiles == 1 and seq_len >= CONV_W - 1
        n_seq = t // seq_len
        z = jnp.zeros((n_seq, seq_len, f), F32)
        h1 = z.at[:, 0].set(halo[:, 1]).reshape(t, nc, fc).transpose(1, 0, 2)
        h2 = z.at[:, 0].set(halo[:, 0]).at[:, 1].set(halo[:, 1]).reshape(t, nc, fc).transpose(1, 0, 2)
        args += [h1, h2]
        in_specs += [_const_spec((nc, tm, fc)), _const_spec((nc, tm, fc))]
        tail_rows = tm
        seq_tiles = 1
    else:
        assert seq_len % tm == 0
        tail_rows = SUBLANE
        seq_tiles = seq_len // tm
        scratch.append(pltpu.VMEM((nc, SUBLANE, fc), F32))
    if g_final is not None:
        args.append(g_final.reshape(1, d))
        in_specs.append(_const_spec((1, d)))
    kern = functools.partial(_ffn_kernel, tm=tm, nc=nc, seq_tiles=seq_tiles, seq_len=seq_len,
                             halo=halo is not None, final_norm=g_final is not None)
    y, gt = pl.pallas_call(
        kern,
        grid=(n_tiles,),
        in_specs=in_specs,
        out_specs=[pl.BlockSpec((tm, d), lambda i: (i, 0)),
                   pl.BlockSpec((1, nc, tail_rows, fc), lambda i: (i, 0, 0, 0))],
        out_shape=[jax.ShapeDtypeStruct((t, d), F32),
                   jax.ShapeDtypeStruct((n_tiles, nc, tail_rows, fc), F32)],
        scratch_shapes=scratch,
        compiler_params=_params("arbitrary"),
        name="conv_ffn",
    )(*args)
    gt = gt.transpose(0, 2, 1, 3).reshape(n_tiles, tail_rows, f)
    return y, gt


S5_LANE_CHUNK = 512
S5_IN_BLOCK = 256
S5_OUT_GROUPS = 8


def _s5_kernel(*refs, tm, nseq, nsteps, chained):
    refs = list(refs)
    (x_ref, gn_ref, bb_ref, lam_ref, pw_ref, cre_ref, cim_ref, d_ref, wglu_ref, bglu_ref) = refs[:10]
    pos = 10
    if not chained:
        s0_ref = refs[pos]
        pos += 1
    y_ref, sfin_ref = refs[pos:pos + 2]
    xp, hp, sre, sim, yacc = refs[pos + 2:pos + 7]
    if chained:
        cin, st_carry = refs[pos + 7:pos + 9]
    n_state = sre.shape[1]
    d_model = x_ref.shape[-1]

    xp[...] = x_ref[0]
    h = _rmsnorm(xp[...], gn_ref[...])
    hp[...] = h.astype(BF16)

    n_in_blocks = d_model // S5_IN_BLOCK
    wcols = n_state // n_in_blocks
    for blk in range(n_in_blocks):
        bu = jnp.dot(hp[:, blk * S5_IN_BLOCK:(blk + 1) * S5_IN_BLOCK], bb_ref[blk],
                     preferred_element_type=F32)
        sre[:, blk * wcols:(blk + 1) * wcols] = bu[:, :wcols]
        sim[:, blk * wcols:(blk + 1) * wcols] = bu[:, wcols:]

    if chained:
        @pl.when(pl.program_id(1) == 0)
        def _():
            st_carry[...] = jnp.zeros_like(st_carry)

    lc = S5_LANE_CHUNK
    for c0 in range(0, n_state, lc):
        cols = slice(c0, c0 + lc)
        lr = jnp.broadcast_to(lam_ref[0:1, cols], (nseq, lc))
        li = jnp.broadcast_to(lam_ref[1:2, cols], (nseq, lc))
        if chained:
            init = (jnp.zeros((nseq, lc), F32), jnp.zeros((nseq, lc), F32))
        else:
            init = (s0_ref[0, :, cols], s0_ref[1, :, cols])

        cr, ci = init
        for i in range(nsteps):
            rows = slice(i * nseq, (i + 1) * nseq)
            cr, ci = lr * cr - li * ci + sre[rows, cols], lr * ci + li * cr + sim[rows, cols]
            sre[rows, cols] = cr
            sim[rows, cols] = ci

        if not chained:
            sfin_ref[0, :, cols] = cr
            sfin_ref[1, :, cols] = ci
        else:
            last = (nsteps - 1) * nseq
            pr_l = pw_ref[0, nsteps - 1:nsteps, cols]
            pi_l = pw_ref[1, nsteps - 1:nsteps, cols]
            c_r = st_carry[0:1, cols]
            c_i = st_carry[1:2, cols]
            for seq in range(nseq):
                cin[0, seq:seq + 1, :] = c_r
                cin[1, seq:seq + 1, :] = c_i
                e_r = sre[last + seq:last + seq + 1, cols]
                e_i = sim[last + seq:last + seq + 1, cols]
                c_r, c_i = pr_l * c_r - pi_l * c_i + e_r, pr_l * c_i + pi_l * c_r + e_i
            st_carry[0:1, cols] = c_r
            st_carry[1:2, cols] = c_i
            in_r = cin[0]
            in_i = cin[1]

            for i in range(nsteps):
                rows = slice(i * nseq, (i + 1) * nseq)
                pr = jnp.broadcast_to(pw_ref[0, i:i + 1, cols], (nseq, lc))
                pi = jnp.broadcast_to(pw_ref[1, i:i + 1, cols], (nseq, lc))
                sre[rows, cols] = sre[rows, cols] + (pr * in_r - pi * in_i)
                sim[rows, cols] = sim[rows, cols] + (pr * in_i + pi * in_r)

    if chained:
        sfin_ref[0] = st_carry[...]

    kcols = S5_OUT_GROUPS * SSM_STATE
    ncols = S5_OUT_GROUPS * SSM_GROUP
    for k in range(n_state // kcols):
        yk = jnp.dot(sre[:, k * kcols:(k + 1) * kcols].astype(BF16), cre_ref[k], preferred_element_type=F32)
        yk += jnp.dot(sim[:, k * kcols:(k + 1) * kcols].astype(BF16), cim_ref[k], preferred_element_type=F32)
        yacc[:, k * ncols:(k + 1) * ncols] = yk
    y = yacc[...] + d_ref[...] * _rmsnorm(xp[...], gn_ref[...])
    z = jnp.dot(jax.nn.gelu(y).astype(BF16), wglu_ref[...], preferred_element_type=F32) + bglu_ref[...]
    out = xp[...] + z[:, :d_model] * jax.nn.sigmoid(z[:, d_model:])
    y_ref[0] = out


def _s5_discretize(a_re, a_im, log_dt, b_re, b_im):
    a_re, a_im = a_re.astype(F32), a_im.astype(F32)
    dt = jnp.exp(log_dt.astype(F32))[:, None]
    mag = jnp.exp(a_re * dt)
    lam_re, lam_im = mag * jnp.cos(a_im * dt), mag * jnp.sin(a_im * dt)
    den = a_re * a_re + a_im * a_im
    n_re, n_im = lam_re - 1.0, lam_im
    f_re = (n_re * a_re + n_im * a_im) / den
    f_im = (n_im * a_re - n_re * a_im) / den
    b_re, b_im = b_re.astype(F32), b_im.astype(F32)
    bb_re = f_re[..., None] * b_re - f_im[..., None] * b_im
    bb_im = f_re[..., None] * b_im + f_im[..., None] * b_re
    return lam_re, lam_im, bb_re, bb_im


def _s5_weights(lam_re, lam_im, bb_re, bb_im, c_re, c_im, nsteps):
    n_groups = lam_re.shape[0]
    n_state = n_groups * SSM_STATE
    pr, pi = [lam_re.reshape(-1)], [lam_im.reshape(-1)]
    for _ in range(nsteps - 1):
        pr, pi = pr + [pr[-1] * pr[0] - pi[-1] * pi[0]], pi + [pr[-1] * pi[0] + pi[-1] * pr[0]]
    pw = jnp.stack([jnp.stack(pr), jnp.stack(pi)])
    lam = jnp.stack([lam_re.reshape(-1), lam_im.reshape(-1)])
    gpb = S5_IN_BLOCK // SSM_GROUP
    nb = n_groups // gpb
    eye = jnp.eye(gpb, dtype=F32)

    def bdiag_in(bb):
        bb = bb.reshape(nb, gpb, SSM_STATE, SSM_GROUP)
        return jnp.einsum('ngpc,gh->ngchp', bb, eye).reshape(nb, gpb * SSM_GROUP, gpb * SSM_STATE)

    bmat = jnp.concatenate([bdiag_in(bb_re), bdiag_in(bb_im)], axis=-1).astype(BF16)
    go = S5_OUT_GROUPS
    eye_o = jnp.eye(go, dtype=F32)

    def bdiag_out(cc):
        cc = cc.astype(F32).reshape(n_groups // go, go, SSM_GROUP, SSM_STATE)
        return jnp.einsum('ngcp,gh->ngphc', cc, eye_o).reshape(n_groups // go, go * SSM_STATE, go * SSM_GROUP)

    return lam, pw, bmat, bdiag_out(c_re).astype(BF16), bdiag_out(-c_im.astype(F32)).astype(BF16)


def s5_layer(x, gn, disc, c_re, c_im, d_skip, w_glu, b_glu, *, tm=None, s0=None):
    b, s, d = x.shape
    lam_re, lam_im, bb_re, bb_im = disc
    n_groups = lam_re.shape[0]
    n_state = n_groups * SSM_STATE
    chained = s0 is None
    if chained:
        nseq, nsteps = SUBLANE, tm // SUBLANE
        grid = (b, s // tm)
        xin = x.reshape(b, s // tm, nseq, nsteps, d).transpose(0, 1, 3, 2, 4).reshape(b, s, d)
        x_spec = pl.BlockSpec((1, tm, d), lambda i, j: (i, j, 0))
        sfin_spec = pl.BlockSpec((1, 2, n_state), lambda i, j: (i, 0, 0))
        sfin_shape = jax.ShapeDtypeStruct((b, 2, n_state), F32)
    else:
        nseq, nsteps, tm = b, s, b * s
        grid = (1, 1)
        xin = x.transpose(1, 0, 2).reshape(1, tm, d)
        x_spec = pl.BlockSpec((1, tm, d), lambda i, j: (0, 0, 0))
        sfin_spec = _const_spec((2, nseq, n_state))
        sfin_shape = jax.ShapeDtypeStruct((2, nseq, n_state), F32)
    lam, pw, bmat, cre, cim = _s5_weights(lam_re, lam_im, bb_re, bb_im, c_re, c_im, nsteps)
    args = [xin, gn.reshape(1, d), bmat, lam, pw, cre, cim, d_skip.reshape(1, d).astype(F32),
            w_glu.astype(BF16), b_glu.reshape(1, 2 * d).astype(F32)]
    in_specs = [x_spec, _const_spec((1, d)), _const_spec(bmat.shape), _const_spec(lam.shape),
                _const_spec(pw.shape), _const_spec(cre.shape), _const_spec(cim.shape),
                _const_spec((1, d)), _const_spec((d, 2 * d)), _const_spec((1, 2 * d))]
    scratch = [pltpu.VMEM((tm, d), F32), pltpu.VMEM((tm, d), BF16), pltpu.VMEM((tm, n_state), F32),
               pltpu.VMEM((tm, n_state), F32), pltpu.VMEM((tm, d), F32)]
    if chained:
        scratch += [pltpu.VMEM((2, nseq, S5_LANE_CHUNK), F32), pltpu.VMEM((2, n_state), F32)]
    else:
        args.append(jnp.stack([s0[0].reshape(b, n_state), s0[1].reshape(b, n_state)]).astype(F32))
        in_specs.append(_const_spec((2, nseq, n_state)))
    kern = functools.partial(_s5_kernel, tm=tm, nseq=nseq, nsteps=nsteps, chained=chained)
    y, sfin = pl.pallas_call(
        kern,
        grid=grid,
        in_specs=in_specs,
        out_specs=[x_spec, sfin_spec],
        out_shape=[jax.ShapeDtypeStruct(xin.shape, F32), sfin_shape],
        scratch_shapes=scratch,
        compiler_params=_params("arbitrary", "arbitrary"),
        name="s5_layer",
    )(*args)
    if chained:
        y = y.reshape(b, s // tm, nsteps, nseq, d).transpose(0, 1, 3, 2, 4).reshape(b, s, d)
        return (y, sfin[:, 0].reshape(b, n_groups, SSM_STATE), sfin[:, 1].reshape(b, n_groups, SSM_STATE))
    y = y.reshape(s, b, d).transpose(1, 0, 2)
    return y, sfin[0].reshape(b, n_groups, SSM_STATE), sfin[1].reshape(b, n_groups, SSM_STATE)


INT_MIN = -2 ** 31
N_KEY_BITS = 32


def _candidate(prefix, i):
    cand = prefix | lax.shift_left(jnp.int32(1), 31 - i)
    u = cand ^ jnp.int32(INT_MIN)
    bits = jnp.where(u >= 0, u, u ^ jnp.int32(0x7FFFFFFF))
    return cand, lax.bitcast_convert_type(bits, F32)


def _key_to_float(prefix):
    u = prefix ^ jnp.int32(INT_MIN)
    return lax.bitcast_convert_type(jnp.where(u >= 0, u, u ^ jnp.int32(0x7FFFFFFF)), F32)


KEY_CHUNK = 512
ATTN_KEY_STEP = KEY_CHUNK // 2
ATTN_SLAB = 64
QK_ROWS = 256
COUNT_ROWS = 128
COUNT_ACCS = 4
HEADS_PER_DOT = 2
LOG2_E = 1.4426950408889634


def _dsa_prompt_kernel(q_ref, k_ref, vt_ref, iq_ref, ik_ref, iw_ref, o_ref, sc, thr_scr, m_scr, l_scr, acc,
                       st_a, st_b, p_buf, *, topk, nkc):
    qb = Q_BLOCK
    kc = KEY_CHUNK
    hpd = HEADS_PER_DOT
    pairs_per_kv = N_HEADS // N_KV_HEADS // hpd
    j = (nkc - 1) * (kc // qb) + pl.program_id(1)
    nkb = j + 1

    def key_rows(kb):
        return pl.ds(pl.multiple_of(kb * qb, qb), qb)

    def chunk_rows(c):
        return slice(c * kc, (c + 1) * kc)

    kpos = lax.broadcasted_iota(jnp.int32, (kc, qb), 0)
    qpos = j * qb + lax.broadcasted_iota(jnp.int32, (kc, qb), 1)

    @pl.when(nkb * qb <= topk)
    def _():
        sc[chunk_rows(0), :] = jnp.where(kpos <= qpos, 0.0, NEG_INF)
        thr_scr[...] = jnp.zeros_like(thr_scr)

    @pl.when(nkb * qb > topk)
    def _():
        iq = iq_ref[0].reshape(N_IDX_HEADS * qb, IDX_DIM)
        iw = iw_ref[0]

        for c in range(nkc):
            dots = lax.dot_general(ik_ref[0, chunk_rows(c), :], iq, (((1,), (1,)), ((), ())),
                                   preferred_element_type=F32)
            s = jnp.zeros((kc, qb), F32)
            for h in range(N_IDX_HEADS):
                s = s + jnp.maximum(dots[:, h * qb:(h + 1) * qb], 0.0) * iw[h:h + 1, :]
            sc[chunk_rows(c), :] = jnp.where(kpos + c * kc <= qpos, s, NEG_INF)

        kf = jnp.float32(topk)

        def count(pred):
            cnt = jnp.zeros((COUNT_ACCS, SUBLANE, qb), F32)
            for c in range(nkc):
                for r0 in range(0, kc, COUNT_ROWS):
                    x = jnp.where(pred(sc[c * kc + r0:c * kc + r0 + COUNT_ROWS, :]), 1.0, 0.0)
                    x = x.reshape(COUNT_ROWS // SUBLANE // COUNT_ACCS, COUNT_ACCS, SUBLANE, qb)
                    cnt = cnt + jnp.sum(x, axis=0)
            return jnp.sum(jnp.sum(cnt, axis=0), axis=0, keepdims=True)

        def bit_body(i, prefix):
            cand, cf = _candidate(prefix, i)
            return jnp.where(count(lambda s: s >= cf) >= kf, cand, prefix)

        thr = _key_to_float(lax.fori_loop(0, N_KEY_BITS, bit_body, jnp.zeros((1, qb), jnp.int32)))
        thr_scr[...] = thr

        @pl.when(jnp.max(count(lambda s: s >= thr)) > kf)
        def _():
            need = kf - count(lambda s: s > thr)
            ki = lax.broadcasted_iota(jnp.int32, (qb, qb), 0)
            kj = lax.broadcasted_iota(jnp.int32, (qb, qb), 1)
            tri = jnp.where(kj <= ki, 1.0, 0.0).astype(BF16)

            def drop_blk(kb, run):
                s = sc[key_rows(kb), :]
                tie = s == thr
                pre = jnp.dot(tri, jnp.where(tie, 1.0, 0.0).astype(BF16), preferred_element_type=F32)
                sc[key_rows(kb), :] = jnp.where(tie & (pre + run > need), NEG_INF, s)
                return run + pre[qb - 1:qb, :]

            lax.fori_loop(0, nkb, drop_blk, jnp.zeros((1, qb), F32))

    m_scr[...] = jnp.full(m_scr.shape, NEG_INF, F32)
    l_scr[...] = jnp.zeros_like(l_scr)
    acc[...] = jnp.zeros_like(acc)
    thr = thr_scr[...]
    ks = ATTN_KEY_STEP
    slab = ATTN_SLAB
    cols = hpd * qb

    def step_rows(key0):
        return slice(key0, key0 + ks)

    def qk(key0, st_buf):
        for s0 in range(0, ks, QK_ROWS):
            rows = slice(key0 + s0, key0 + s0 + QK_ROWS)
            b1 = jnp.where(sc[rows, :] >= thr, 0.0, NEG_INF)
            bt = jnp.concatenate([b1] * hpd, axis=1)
            for g in range(N_KV_HEADS):
                kg = k_ref[0, g, rows, :]
                for pr in range(pairs_per_kv):
                    hp = g * pairs_per_kv + pr
                    qp = q_ref[0, hp * hpd:(hp + 1) * hpd].reshape(hpd * qb, HEAD_DIM)
                    st_buf[hp, s0:s0 + QK_ROWS, :] = lax.dot_general(
                        kg, qp, (((1,), (1,)), ((), ())), preferred_element_type=F32) + bt

    def slabs(ref, hp):
        for s0 in range(0, ks, slab):
            yield s0, ref[hp, s0:s0 + slab, :]

    def fold(x, op):
        x = x.reshape(slab // SUBLANE, SUBLANE, cols)
        return op(x, axis=0)

    def softmax_pv(key0, st_buf):
        for g in range(N_KV_HEADS):
            vtg = vt_ref[0, g, key0 // ks]
            for pr in range(pairs_per_kv):
                hp = g * pairs_per_kv + pr
                m_old = m_scr[hp]
                m8 = None
                for _, st in slabs(st_buf, hp):
                    f = fold(st, jnp.max)
                    m8 = f if m8 is None else jnp.maximum(m8, f)
                m_new = jnp.maximum(m_old, jnp.max(m8, axis=0, keepdims=True))
                alpha = jnp.exp2(m_old - m_new)
                l8 = jnp.zeros((SUBLANE, cols), F32)
                for s0, st in slabs(st_buf, hp):
                    p = jnp.exp2(st - m_new)
                    l8 = l8 + fold(p, jnp.sum)
                    p_buf[hp, s0:s0 + slab, :] = p.astype(BF16)
                l_scr[hp] = alpha * l_scr[hp] + jnp.sum(l8, axis=0, keepdims=True)
                acc[hp] = alpha * acc[hp] + jnp.dot(vtg, p_buf[hp], preferred_element_type=F32)
                m_scr[hp] = m_new

    qk(0, st_a)
    for c in range(nkc):
        key0 = c * kc
        qk(key0 + ks, st_b)
        softmax_pv(key0, st_a)
        if c + 1 < nkc:
            qk(key0 + 2 * ks, st_a)
        softmax_pv(key0 + ks, st_b)
    for hp in range(N_HEADS // hpd):
        ot = acc[hp] / l_scr[hp]
        for i in range(hpd):
            hd = hp * hpd + i
            o_ref[0, :, hd * HEAD_DIM:(hd + 1) * HEAD_DIM] = ot[:, i * qb:(i + 1) * qb].T.astype(o_ref.dtype)


def dsa_prompt_attention(q, k, v, iq, ik, iw):
    b, s = q.shape[:2]
    topk = min(TOPK_MAX, s // 4)
    assert topk % Q_BLOCK == 0 and topk <= KEY_CHUNK and s % KEY_CHUNK == 0
    q_hm = (q * (HEAD_DIM ** -0.5 * LOG2_E)).astype(BF16).transpose(0, 2, 1, 3)
    k_hm = k.astype(BF16).transpose(0, 2, 1, 3)
    n_steps = s // ATTN_KEY_STEP
    vt = v.astype(BF16).reshape(b, n_steps, ATTN_KEY_STEP, N_KV_HEADS, HEAD_DIM).transpose(0, 3, 1, 4, 2)
    iq_hm = iq.astype(BF16).transpose(0, 2, 1, 3)
    iw_t = (iw * (IDX_DIM ** -0.5 * N_IDX_HEADS ** -0.5)).transpose(0, 2, 1)
    return dsa_prompt_attention_from_layouts((q_hm, k_hm, vt, iq_hm, ik.astype(BF16), iw_t), b, s)


def dsa_prompt_attention_from_layouts(args, b, s):
    topk = min(TOPK_MAX, s // 4)
    assert topk % Q_BLOCK == 0 and topk <= KEY_CHUNK and s % KEY_CHUNK == 0
    n_dots = N_HEADS // HEADS_PER_DOT
    dot_cols = HEADS_PER_DOT * Q_BLOCK
    bpc = KEY_CHUNK // Q_BLOCK
    steps_per_chunk = KEY_CHUNK // ATTN_KEY_STEP
    return jnp.concatenate([_dsa_prompt_call(args, b, topk, nkc, bpc, steps_per_chunk, n_dots, dot_cols)
                            for nkc in range(1, s // KEY_CHUNK + 1)], axis=1)


def _dsa_prompt_call(args, b, topk, nkc, bpc, steps_per_chunk, n_dots, dot_cols):
    nk = nkc * KEY_CHUNK
    j0 = (nkc - 1) * bpc
    return pl.pallas_call(
        functools.partial(_dsa_prompt_kernel, topk=topk, nkc=nkc),
        grid=(b, bpc),
        in_specs=[
            pl.BlockSpec((1, N_HEADS, Q_BLOCK, HEAD_DIM), lambda i, j: (i, 0, j0 + j, 0)),
            pl.BlockSpec((1, N_KV_HEADS, nk, HEAD_DIM), lambda i, j: (i, 0, 0, 0)),
            pl.BlockSpec((1, N_KV_HEADS, nkc * steps_per_chunk, HEAD_DIM, ATTN_KEY_STEP),
                         lambda i, j: (i, 0, 0, 0, 0)),
            pl.BlockSpec((1, N_IDX_HEADS, Q_BLOCK, IDX_DIM), lambda i, j: (i, 0, j0 + j, 0)),
            pl.BlockSpec((1, nk, IDX_DIM), lambda i, j: (i, 0, 0)),
            pl.BlockSpec((1, args[5].shape[1], Q_BLOCK), lambda i, j: (i, 0, j0 + j)),
        ],
        out_specs=pl.BlockSpec((1, Q_BLOCK, N_HEADS * HEAD_DIM), lambda i, j: (i, j, 0)),
        out_shape=jax.ShapeDtypeStruct((b, bpc * Q_BLOCK, N_HEADS * HEAD_DIM), BF16),
        scratch_shapes=[pltpu.VMEM((nk, Q_BLOCK), F32), pltpu.VMEM((1, Q_BLOCK), F32),
                        pltpu.VMEM((n_dots, 1, dot_cols), F32),
                        pltpu.VMEM((n_dots, 1, dot_cols), F32),
                        pltpu.VMEM((n_dots, HEAD_DIM, dot_cols), F32),
                        pltpu.VMEM((n_dots, ATTN_KEY_STEP, dot_cols), F32),
                        pltpu.VMEM((n_dots, ATTN_KEY_STEP, dot_cols), F32),
                        pltpu.VMEM((n_dots, ATTN_KEY_STEP, dot_cols), BF16)],
        compiler_params=_params("arbitrary", "arbitrary"),
        name="dsa_prompt",
    )(*args)


PAGES_PER_CHUNK = 16
T_PAD = SUBLANE


IDX_RING_SLOTS = 8
KV_RING_SLOTS = 16
SELECT_BATCH = 16


def _page_ring(pt_ref, streams, nch, buf, sem, dst_of_page, n_batches):
    cpp = PAGES_PER_CHUNK
    n = len(streams) * nch
    slots = buf.shape[0]
    assert n % slots == 0
    b = pl.program_id(0)

    def copies(batch, i):
        src, c, slot = streams[i // nch], i % nch, i % slots
        return [pltpu.make_async_copy(src.at[pt_ref[batch, c * cpp + p]], dst_of_page(buf, slot, p),
                                      sem.at[slot]) for p in range(cpp)]

    def request(i):
        if i < n:
            for cp in copies(b, i):
                cp.start()
        else:
            @pl.when(b + 1 < n_batches)
            def _():
                for cp in copies(b + 1, i - n):
                    cp.start()

    def advance(i):
        for cp in copies(b, i):
            cp.wait()
        request(i + slots - 1)
        return i % slots

    def prologue():
        @pl.when(b == 0)
        def _():
            for i in range(slots - 1):
                request(i)

    return advance, prologue


def _sample_scores_kernel(pt_ref, iq_ref, iw_ref, iknew_ref, cik_hbm, sc_ref, ikbuf, sem,
                          *, n_pages, n_new, n_batches):
    nch = n_pages // PAGES_PER_CHUNK
    cw = PAGES_PER_CHUNK * PAGE_SIZE
    advance, prologue = _page_ring(
        pt_ref, [cik_hbm], nch, ikbuf, sem,
        lambda buf, slot, p: buf.at[slot, :, pl.ds(p * PAGE_SIZE, PAGE_SIZE)], n_batches)
    iq = iq_ref[0]
    iw = iw_ref[0]

    def scores_of(ikc_t):
        dots = jnp.dot(iq, ikc_t, preferred_element_type=F32)
        s = jnp.zeros((T_PAD, ikc_t.shape[1]), F32)
        for h in range(N_IDX_HEADS):
            w = iw[h * T_PAD:(h + 1) * T_PAD, 0:1]
            s = s + jnp.maximum(dots[h * T_PAD:(h + 1) * T_PAD, :], 0.0) * w
        return s

    prologue()
    for i in range(nch):
        slot = advance(i)
        sc_ref[0, i] = scores_of(ikbuf[slot].astype(BF16))
    kcol = lax.broadcasted_iota(jnp.int32, (T_PAD, PAGE_SIZE), 1)
    trow = lax.broadcasted_iota(jnp.int32, (T_PAD, PAGE_SIZE), 0)
    new_ok = (kcol <= trow) & (kcol < n_new)
    sc_ref[0, nch] = jnp.full((T_PAD, cw), NEG_INF, F32)
    sc_ref[0, nch, :, 0:PAGE_SIZE] = jnp.where(new_ok, scores_of(iknew_ref[0]), NEG_INF)


def _sample_select_kernel(sc_ref, sc2_ref, thr_ref, *, n_new, topk):
    sb, nblk, _, cw = sc_ref.shape
    kf = jnp.float32(topk)

    def count(pred):
        def blk(c, cnt):
            for t in range(cw // LANE):
                cnt = cnt + jnp.where(pred(sc_ref[:, c, :, t * LANE:(t + 1) * LANE]), 1.0, 0.0)
            return cnt

        cnt = lax.fori_loop(0, nblk, blk, jnp.zeros((sb, T_PAD, LANE), F32))
        return jnp.sum(cnt, axis=2, keepdims=True)

    def bit_body(i, prefix):
        cand, cf = _candidate(prefix, i)
        return jnp.where(count(lambda s: s >= cf) >= kf, cand, prefix)

    thr = _key_to_float(lax.fori_loop(0, N_KEY_BITS, bit_body, jnp.zeros((sb, T_PAD, 1), jnp.int32)))
    thr_ref[...] = jnp.broadcast_to(thr, thr_ref.shape)

    def copy_blk(c, _):
        sc2_ref[:, c] = sc_ref[:, c]
        return 0

    lax.fori_loop(0, nblk, copy_blk, 0)
    real = lax.broadcasted_iota(jnp.int32, (sb, T_PAD, 1), 1) < n_new

    @pl.when(jnp.max(jnp.where(real, count(lambda s: s >= thr), 0.0)) > kf)
    def _():
        need = kf - count(lambda s: s > thr)
        ki = lax.broadcasted_iota(jnp.int32, (LANE, LANE), 0)
        kj = lax.broadcasted_iota(jnp.int32, (LANE, LANE), 1)
        tri = jnp.where(ki <= kj, 1.0, 0.0).astype(BF16)

        def drop_blk(c, run):
            s = sc_ref[:, c]
            for t in range(cw // LANE):
                st = s[:, :, t * LANE:(t + 1) * LANE]
                tie = st == thr
                ind = jnp.where(tie, 1.0, 0.0).reshape(sb * T_PAD, LANE).astype(BF16)
                pre = jnp.dot(ind, tri, preferred_element_type=F32).reshape(sb, T_PAD, LANE)
                sc2_ref[:, c, :, t * LANE:(t + 1) * LANE] = jnp.where(tie & (pre + run > need), NEG_INF, st)
                run = run + pre[:, :, LANE - 1:LANE]
            return run

        lax.fori_loop(0, nblk, drop_blk, jnp.zeros((sb, T_PAD, 1), F32))


def _sample_attend_kernel(pt_ref, q_ref, knew_ref, vnew_ref, sc_ref, thr_ref, ck_hbm, cv_hbm, o_ref,
                          kvbuf, lg, sem, *, n_pages, n_batches):
    nch = n_pages // PAGES_PER_CHUNK
    cw = PAGES_PER_CHUNK * PAGE_SIZE
    rep = N_HEADS // N_KV_HEADS
    rows = rep * T_PAD
    page_rows = PAGE_SIZE * N_KV_HEADS
    advance, prologue = _page_ring(
        pt_ref, [ck_hbm, cv_hbm], nch, kvbuf, sem,
        lambda buf, slot, p: buf.at[slot, pl.ds(p * page_rows, page_rows), :], n_batches)
    thr = thr_ref[0][:, 0:1]

    def kv_rows(slot, g):
        return kvbuf[slot, pl.ds(g, cw, stride=N_KV_HEADS), :].astype(BF16)

    def logits_of(kg, g, s):
        qg = q_ref[0, g]
        lgt = lax.dot_general(qg, kg, (((1,), (1,)), ((), ())), preferred_element_type=F32)
        return lgt + jnp.concatenate([jnp.where(s >= thr, 0.0, NEG_INF)] * rep, axis=0)

    prologue()
    for i in range(nch):
        slot = advance(i)
        for g in range(N_KV_HEADS):
            lg[g, i] = logits_of(kv_rows(slot, g), g, sc_ref[0, i])
    for g in range(N_KV_HEADS):
        lg[g, nch] = jnp.full((rows, cw), NEG_INF, F32)
        lg[g, nch, :, 0:PAGE_SIZE] = logits_of(knew_ref[0][:, g * HEAD_DIM:(g + 1) * HEAD_DIM], g,
                                               sc_ref[0, nch, :, 0:PAGE_SIZE])

    denom = []
    for g in range(N_KV_HEADS):
        mx = lax.fori_loop(0, nch + 1, lambda c, m, g=g: jnp.maximum(m, lg[g, c]),
                           jnp.full((rows, cw), NEG_INF, F32))
        mx = jnp.max(mx, axis=1, keepdims=True)

        def exp_blk(c, tot, g=g, mx=mx):
            p = jnp.exp(lg[g, c] - mx)
            lg[g, c] = p
            return tot + p

        tot = lax.fori_loop(0, nch + 1, exp_blk, jnp.zeros((rows, cw), F32))
        denom.append(jnp.sum(tot, axis=1, keepdims=True))
        o_ref[0, g] = jnp.dot(lg[g, nch, :, 0:PAGE_SIZE].astype(BF16),
                              vnew_ref[0][:, g * HEAD_DIM:(g + 1) * HEAD_DIM], preferred_element_type=F32)

    for i in range(nch, 2 * nch):
        slot = advance(i)
        for g in range(N_KV_HEADS):
            o_ref[0, g] += jnp.dot(lg[g, i - nch].astype(BF16), kv_rows(slot, g),
                                   preferred_element_type=F32)
    for g in range(N_KV_HEADS):
        o_ref[0, g] = o_ref[0, g] / denom[g]


def dsa_sample_attention(q, k, v, iq, ik, iw, cache_k, cache_v, cache_idx_k, page_table):
    b, t = q.shape[:2]
    n_pages = page_table.shape[1]
    n_phys = cache_k.shape[0]
    n_keys = n_pages * PAGE_SIZE + t
    topk = min(TOPK_MAX, n_keys // 4)
    rep = N_HEADS // N_KV_HEADS
    assert t <= T_PAD and n_pages % PAGES_PER_CHUNK == 0 and n_pages * PAGE_SIZE >= topk
    nch = n_pages // PAGES_PER_CHUNK
    cw = PAGES_PER_CHUNK * PAGE_SIZE

    def pad_t(x):
        return jnp.pad(x, ((0, 0), (0, T_PAD - t)) + ((0, 0),) * (x.ndim - 2))

    def pad_page(x):
        return jnp.pad(x, ((0, 0), (0, PAGE_SIZE - t), (0, 0)))

    kvc = N_KV_HEADS * HEAD_DIM
    q_s = pad_t((q * HEAD_DIM ** -0.5).astype(BF16)).reshape(b, T_PAD, N_KV_HEADS, rep, HEAD_DIM)
    q_s = q_s.transpose(0, 2, 3, 1, 4).reshape(b, N_KV_HEADS, rep * T_PAD, HEAD_DIM)
    iq_s = pad_t(iq.astype(BF16)).transpose(0, 2, 1, 3).reshape(b, N_IDX_HEADS * T_PAD, IDX_DIM)
    iw_s = pad_t(iw * (IDX_DIM ** -0.5 * N_IDX_HEADS ** -0.5)).transpose(0, 2, 1)
    iw_s = jnp.broadcast_to(iw_s.reshape(b, N_IDX_HEADS * T_PAD, 1), (b, N_IDX_HEADS * T_PAD, LANE))
    k_new = pad_page(k.reshape(b, t, kvc).astype(BF16))
    v_new = pad_page(v.reshape(b, t, kvc).astype(BF16))
    ik_new = pad_page(ik.astype(BF16)).transpose(0, 2, 1)

    def bspec(shape):
        nd = len(shape)
        return pl.BlockSpec((1,) + shape, lambda i, pt: (i,) + (0,) * nd)

    any_spec = pl.BlockSpec(memory_space=pl.ANY)
    sc_shape = (nch + 1, T_PAD, cw)
    sc = pl.pallas_call(
        functools.partial(_sample_scores_kernel, n_pages=n_pages, n_new=t, n_batches=b),
        grid_spec=pltpu.PrefetchScalarGridSpec(
            num_scalar_prefetch=1,
            grid=(b,),
            in_specs=[bspec((N_IDX_HEADS * T_PAD, IDX_DIM)), bspec((N_IDX_HEADS * T_PAD, LANE)),
                      bspec((IDX_DIM, PAGE_SIZE)), any_spec],
            out_specs=bspec(sc_shape),
            scratch_shapes=[pltpu.VMEM((IDX_RING_SLOTS, IDX_DIM, cw), F32),
                            pltpu.SemaphoreType.DMA((IDX_RING_SLOTS,))],
        ),
        out_shape=jax.ShapeDtypeStruct((b,) + sc_shape, F32),
        compiler_params=_params("arbitrary"),
        name="dsa_sample_scores",
    )(page_table, iq_s, iw_s, ik_new, cache_idx_k.transpose(0, 2, 1))

    sb = math.gcd(b, SELECT_BATCH)
    sc_blk = pl.BlockSpec((sb,) + sc_shape, lambda i: (i, 0, 0, 0))
    sc, thr = pl.pallas_call(
        functools.partial(_sample_select_kernel, n_new=t, topk=topk),
        grid=(b // sb,),
        in_specs=[sc_blk],
        out_specs=[sc_blk, pl.BlockSpec((sb, T_PAD, LANE), lambda i: (i, 0, 0))],
        out_shape=[jax.ShapeDtypeStruct((b,) + sc_shape, F32), jax.ShapeDtypeStruct((b, T_PAD, LANE), F32)],
        compiler_params=_params("arbitrary"),
        name="dsa_sample_select",
    )(sc)

    o = pl.pallas_call(
        functools.partial(_sample_attend_kernel, n_pages=n_pages, n_batches=b),
        grid_spec=pltpu.PrefetchScalarGridSpec(
            num_scalar_prefetch=1,
            grid=(b,),
            in_specs=[bspec((N_KV_HEADS, rep * T_PAD, HEAD_DIM)), bspec((PAGE_SIZE, kvc)), bspec((PAGE_SIZE, kvc)),
                      bspec(sc_shape), bspec((T_PAD, LANE)), any_spec, any_spec],
            out_specs=bspec((N_KV_HEADS, rep * T_PAD, HEAD_DIM)),
            scratch_shapes=[pltpu.VMEM((KV_RING_SLOTS, cw * N_KV_HEADS, HEAD_DIM), F32),
                            pltpu.VMEM((N_KV_HEADS, nch + 1, rep * T_PAD, cw), F32),
                            pltpu.SemaphoreType.DMA((KV_RING_SLOTS,))],
        ),
        out_shape=jax.ShapeDtypeStruct((b, N_KV_HEADS, rep * T_PAD, HEAD_DIM), F32),
        compiler_params=_params("arbitrary"),
        name="dsa_sample_attend",
    )(page_table, q_s, k_new, v_new, sc, thr,
      cache_k.reshape(n_phys, PAGE_SIZE * N_KV_HEADS, HEAD_DIM),
      cache_v.reshape(n_phys, PAGE_SIZE * N_KV_HEADS, HEAD_DIM))
    o = o.reshape(b, N_KV_HEADS, rep, T_PAD, HEAD_DIM)[:, :, :, :t]
    return o.transpose(0, 3, 1, 2, 4).reshape(b, t, N_HEADS * HEAD_DIM).astype(BF16)


Q_COLS = N_HEADS * HEAD_DIM
KV_COLS = N_KV_HEADS * HEAD_DIM
IQ_COLS = N_IDX_HEADS * IDX_DIM
IN_COLS = Q_COLS + 2 * KV_COLS + IQ_COLS + IDX_DIM + N_IDX_HEADS
PROMPT_TILE = 512
S5_TILE = 512


def _attn_in_proj_kernel(x_ref, g_ref, w_ref, q_ref, k_ref, v_ref, kh_ref, vt_ref, iq_ref, ik_ref, ikb_ref, iw_ref):
    tm = x_ref.shape[0]
    h = _rmsnorm(x_ref[...], g_ref[...])
    proj = jnp.dot(h.astype(BF16), w_ref[...], preferred_element_type=F32)
    for hd in range(N_HEADS):
        q_ref[0, hd] = (proj[:, hd * HEAD_DIM:(hd + 1) * HEAD_DIM] * (HEAD_DIM ** -0.5 * LOG2_E)).astype(BF16)
    k = proj[:, Q_COLS:Q_COLS + KV_COLS]
    v = proj[:, Q_COLS + KV_COLS:Q_COLS + 2 * KV_COLS]
    for g in range(N_KV_HEADS):
        k_ref[pl.ds(g, tm, stride=N_KV_HEADS), :] = k[:, g * HEAD_DIM:(g + 1) * HEAD_DIM]
        v_ref[pl.ds(g, tm, stride=N_KV_HEADS), :] = v[:, g * HEAD_DIM:(g + 1) * HEAD_DIM]
        kh_ref[0, g] = k[:, g * HEAD_DIM:(g + 1) * HEAD_DIM].astype(BF16)
        for r in range(tm // ATTN_KEY_STEP):
            blk = v[r * ATTN_KEY_STEP:(r + 1) * ATTN_KEY_STEP, g * HEAD_DIM:(g + 1) * HEAD_DIM]
            vt_ref[0, g, r] = blk.T.astype(BF16)
    o = Q_COLS + 2 * KV_COLS
    for hd in range(N_IDX_HEADS):
        iq_ref[0, hd] = proj[:, o + hd * IDX_DIM:o + (hd + 1) * IDX_DIM].astype(BF16)
    ik = proj[:, o + IQ_COLS:o + IQ_COLS + IDX_DIM]
    ik_ref[...] = ik
    ikb_ref[0] = ik.astype(BF16)
    iw = proj[:, o + IQ_COLS + IDX_DIM:o + IQ_COLS + IDX_DIM + LANE] * (IDX_DIM ** -0.5 * N_IDX_HEADS ** -0.5)
    iw_ref[0] = iw.T[0:SUBLANE, :]


def attn_in_proj_prompt(x, g, w_in, b, s, tm):
    t, d = x.shape
    n_pad = Q_COLS + 2 * KV_COLS + IQ_COLS + IDX_DIM + LANE
    w = jnp.pad(w_in, ((0, 0), (0, n_pad - w_in.shape[1]))).astype(BF16)
    tiles = s // tm
    n_steps = s // ATTN_KEY_STEP
    assert s % tm == 0 and tm % ATTN_KEY_STEP == 0

    def bi(i):
        return i // tiles, i % tiles

    outs = pl.pallas_call(
        _attn_in_proj_kernel,
        grid=(t // tm,),
        in_specs=[pl.BlockSpec((tm, d), lambda i: (i, 0)), _const_spec((1, d)), _const_spec((d, n_pad))],
        out_specs=[
            pl.BlockSpec((1, N_HEADS, tm, HEAD_DIM), lambda i: (bi(i)[0], 0, bi(i)[1], 0)),
            pl.BlockSpec((tm * N_KV_HEADS, HEAD_DIM), lambda i: (i, 0)),
            pl.BlockSpec((tm * N_KV_HEADS, HEAD_DIM), lambda i: (i, 0)),
            pl.BlockSpec((1, N_KV_HEADS, tm, HEAD_DIM), lambda i: (bi(i)[0], 0, bi(i)[1], 0)),
            pl.BlockSpec((1, N_KV_HEADS, tm // ATTN_KEY_STEP, HEAD_DIM, ATTN_KEY_STEP),
                         lambda i: (bi(i)[0], 0, bi(i)[1], 0, 0)),
            pl.BlockSpec((1, N_IDX_HEADS, tm, IDX_DIM), lambda i: (bi(i)[0], 0, bi(i)[1], 0)),
            pl.BlockSpec((tm, IDX_DIM), lambda i: (i, 0)),
            pl.BlockSpec((1, tm, IDX_DIM), lambda i: (bi(i)[0], bi(i)[1], 0)),
            pl.BlockSpec((1, SUBLANE, tm), lambda i: (bi(i)[0], 0, bi(i)[1])),
        ],
        out_shape=[
            jax.ShapeDtypeStruct((b, N_HEADS, s, HEAD_DIM), BF16),
            jax.ShapeDtypeStruct((t * N_KV_HEADS, HEAD_DIM), F32),
            jax.ShapeDtypeStruct((t * N_KV_HEADS, HEAD_DIM), F32),
            jax.ShapeDtypeStruct((b, N_KV_HEADS, s, HEAD_DIM), BF16),
            jax.ShapeDtypeStruct((b, N_KV_HEADS, n_steps, HEAD_DIM, ATTN_KEY_STEP), BF16),
            jax.ShapeDtypeStruct((b, N_IDX_HEADS, s, IDX_DIM), BF16),
            jax.ShapeDtypeStruct((t, IDX_DIM), F32),
            jax.ShapeDtypeStruct((b, s, IDX_DIM), BF16),
            jax.ShapeDtypeStruct((b, SUBLANE, s), F32),
        ],
        compiler_params=_params("parallel"),
        name="attn_in_proj",
    )(x, g.reshape(1, d), w)
    q_hm, k, v, k_hm, vt, iq_hm, ik, ik_b, iw_t = outs
    return k, v, ik, (q_hm, k_hm, vt, iq_hm, ik_b, iw_t)


def _split_attn(proj, b, t):
    o = 0
    out = []
    for width, shape in ((Q_COLS, (N_HEADS, HEAD_DIM)), (KV_COLS, (N_KV_HEADS, HEAD_DIM)),
                         (KV_COLS, (N_KV_HEADS, HEAD_DIM)), (IQ_COLS, (N_IDX_HEADS, IDX_DIM)),
                         (IDX_DIM, (IDX_DIM,)), (N_IDX_HEADS, (N_IDX_HEADS,))):
        out.append(proj[:, o:o + width].reshape((b, t) + shape))
        o += width
    return out


def kernel(x_prompt, x_sample, cache_k, cache_v, cache_idx_k, state_ssm_re, state_ssm_im, state_ffn_conv,
           page_table, w_attn_in, w_attn_out, ssm_a_re, ssm_a_im, ssm_log_dt, ssm_b_re, ssm_b_im, ssm_c_re,
           ssm_c_im, ssm_d, w_glu, b_glu, norm_mixer, norm_ffn, w_ffn_up, ffn_conv_w, ffn_conv_b, w_ffn_down,
           norm_final):
    bp, sp, d = x_prompt.shape
    bs, ss, _ = x_sample.shape
    tp, ts = bp * sp, bs * ss
    xp = x_prompt.reshape(tp, d)
    xs = x_sample.reshape(ts, d)

    n_in = w_attn_in.shape[1]
    n_in_pad = -(-n_in // LANE) * LANE
    w_in = jnp.pad(w_attn_in, ((0, 0), (0, n_in_pad - n_in))).astype(BF16)
    w_out = w_attn_out.astype(BF16)
    k_p, v_p, ik_p, prompt_layouts = attn_in_proj_prompt(xp, norm_mixer[0], w_attn_in, bp, sp, PROMPT_TILE)
    k_p = k_p.reshape(bp, sp, N_KV_HEADS, HEAD_DIM)
    v_p = v_p.reshape(bp, sp, N_KV_HEADS, HEAD_DIM)
    ik_p = ik_p.reshape(bp, sp, IDX_DIM)
    q_s, k_s, v_s, iq_s, ik_s, iw_s = _split_attn(norm_matmul(xs, norm_mixer[0], w_in, ts), bs, ss)
    o_p = dsa_prompt_attention_from_layouts(prompt_layouts, bp, sp)
    o_s = dsa_sample_attention(q_s, k_s, v_s, iq_s, ik_s, iw_s, cache_k, cache_v, cache_idx_k, page_table)
    xp = matmul_residual(o_p.reshape(tp, Q_COLS), w_out, xp, PROMPT_TILE)
    xs = matmul_residual(o_s.reshape(ts, Q_COLS), w_out, xs, ts)

    tiles_per_seq = sp // PROMPT_TILE
    xp, gt_p0 = conv_ffn(xp, norm_ffn[0], w_ffn_up[0], ffn_conv_w[0], ffn_conv_b[0], w_ffn_down[0],
                         tm=PROMPT_TILE, seq_len=sp)
    xs, gt_s0 = conv_ffn(xs, norm_ffn[0], w_ffn_up[0], ffn_conv_w[0], ffn_conv_b[0], w_ffn_down[0],
                         tm=ts, seq_len=ss, halo=state_ffn_conv[0])

    disc = _s5_discretize(ssm_a_re, ssm_a_im, ssm_log_dt, ssm_b_re, ssm_b_im)
    xp3, sre_p, sim_p = s5_layer(xp.reshape(bp, sp, d), norm_mixer[1], disc, ssm_c_re, ssm_c_im, ssm_d,
                                 w_glu, b_glu, tm=S5_TILE)
    xs3, sre_s, sim_s = s5_layer(xs.reshape(bs, ss, d), norm_mixer[1], disc, ssm_c_re, ssm_c_im, ssm_d,
                                 w_glu, b_glu, s0=(state_ssm_re, state_ssm_im))

    yp, gt_p1 = conv_ffn(xp3.reshape(tp, d), norm_ffn[1], w_ffn_up[1], ffn_conv_w[1], ffn_conv_b[1],
                         w_ffn_down[1], tm=PROMPT_TILE, seq_len=sp, g_final=norm_final)
    ys, gt_s1 = conv_ffn(xs3.reshape(ts, d), norm_ffn[1], w_ffn_up[1], ffn_conv_w[1], ffn_conv_b[1],
                         w_ffn_down[1], tm=ts, seq_len=ss, halo=state_ffn_conv[1], g_final=norm_final)

    f = w_ffn_down.shape[1]
    keep = CONV_W - 1

    def conv_state_p(gt):
        return gt.reshape(bp, tiles_per_seq, SUBLANE, f)[:, -1, SUBLANE - keep:]

    def conv_state_s(gt):
        return gt.reshape(bs, ss, f)[:, ss - keep:]

    conv_prompt = jnp.stack([conv_state_p(gt_p0), conv_state_p(gt_p1)])
    conv_sample = jnp.stack([conv_state_s(gt_s0), conv_state_s(gt_s1)])
    return (yp.reshape(bp, sp, d), ys.reshape(bs, ss, d), k_p, v_p, ik_p, k_s, v_s, ik_s,
            sre_p, sim_p, sre_s, sim_s, conv_prompt, conv_sample)
```
